```python
import math
import jax, jax.numpy as jnp
from jax import lax
import numpy as np

D_MODEL = 1024
BATCH = 32
SEQ = 2048
DEPTH = 4

N_A_LAYERS = (DEPTH + 1) // 2
N_B_LAYERS = DEPTH // 2
N_DENSE = (DEPTH + 1) // 2
N_MOE = DEPTH // 2

MLA_HEADS = 8
QK_NOPE = 128
QK_ROPE = 64
V_HEAD = 128
Q_LORA = 384
KV_LORA = 256
ROPE_THETA = 10000.0
Q_BLOCK = 128

SWA_HEADS = 16
SWA_KV_HEADS = 2
SWA_HEAD_DIM = 64
WINDOW = 128
BLOCK = 128

REL_BUCKETS = 32
REL_MAX_DIST = 128

D_FF = 2816
N_EXPERTS = 8
TOP_K = 2
D_FF_EXPERT = 3584

EPS = 1e-6
NEG_INF = -1e30

kernel_name = "yoco_mla_swa_sink_t5_moe_trunk"


def rmsnorm(x, g):
    xf = x.astype(jnp.float32)
    y = xf * lax.rsqrt(jnp.mean(xf * xf, axis=-1, keepdims=True) + EPS)
    return (y * g.astype(jnp.float32)).astype(x.dtype)


def rope(x, positions):
    half = x.shape[-1] // 2
    inv = ROPE_THETA ** (-jnp.arange(half, dtype=jnp.float32) / half)
    ang = positions.astype(jnp.float32)[..., None] * inv
    if x.ndim == 4:
        ang = ang[:, :, None, :]
    cos, sin = jnp.cos(ang), jnp.sin(ang)
    xf = x.astype(jnp.float32)
    x1, x2 = xf[..., :half], xf[..., half:]
    return jnp.concatenate([x1 * cos - x2 * sin, x2 * cos + x1 * sin], axis=-1).astype(x.dtype)


def t5_bucket(dist):
    n = jnp.maximum(dist, 0)
    max_exact = REL_BUCKETS // 2
    nf = jnp.maximum(n, 1).astype(jnp.float32)
    large = max_exact + (jnp.log(nf / max_exact) / math.log(REL_MAX_DIST / max_exact)
                         * (REL_BUCKETS - max_exact)).astype(jnp.int32)
    large = jnp.minimum(large, REL_BUCKETS - 1)
    return jnp.where(n < max_exact, n, large)


def mla(xn, positions, w_in, g_q, g_kv, w_uq, w_ukv, w_o):
    B, S, _ = xn.shape
    lat = xn @ w_in
    c_q = rmsnorm(lat[..., :Q_LORA], g_q)
    c_kv = rmsnorm(lat[..., Q_LORA:Q_LORA + KV_LORA], g_kv)
    k_rope = rope(lat[..., Q_LORA + KV_LORA:], positions)
    q = (c_q @ w_uq).reshape(B, S, MLA_HEADS, QK_NOPE + QK_ROPE)
    q_nope = q[..., :QK_NOPE]
    q_rope = rope(q[..., QK_NOPE:], positions)
    kv = (c_kv @ w_ukv).reshape(B, S, MLA_HEADS, QK_NOPE + V_HEAD)
    k_nope, v = kv[..., :QK_NOPE], kv[..., QK_NOPE:]
    scale = (QK_NOPE + QK_ROPE) ** -0.5
    nb = S // Q_BLOCK
    k_idx = jnp.arange(S)

    def to_blocks(t):
        return jnp.moveaxis(t.reshape(B, nb, Q_BLOCK, *t.shape[2:]), 1, 0)

    def block(args):
        qn, qr, i = args
        s = (jnp.einsum('bqhd,bkhd->bhqk', qn, k_nope, preferred_element_type=jnp.float32)
             + jnp.einsum('bqhr,bkr->bhqk', qr, k_rope, preferred_element_type=jnp.float32)) * scale
        q_idx = i * Q_BLOCK + jnp.arange(Q_BLOCK)
        s = jnp.where(k_idx[None, :] <= q_idx[:, None], s, NEG_INF)
        p = jax.nn.softmax(s, axis=-1).astype(v.dtype)
        return jnp.einsum('bhqk,bkhd->bqhd', p, v)

    o = lax.map(block, (to_blocks(q_nope), to_blocks(q_rope), jnp.arange(nb)))
    o = jnp.moveaxis(o, 0, 1).reshape(B, S, MLA_HEADS * V_HEAD)
    return o @ w_o


def shared_band_kv(h, positions, kv_g, kv_w, kv_b):
    B, S, _ = h.shape
    nb = S // BLOCK
    kv = rmsnorm(h, kv_g) @ kv_w + kv_b
    kvw = SWA_KV_HEADS * SWA_HEAD_DIM
    k = kv[..., :kvw].reshape(B, S, SWA_KV_HEADS, SWA_HEAD_DIM)
    v = kv[..., kvw:].reshape(B, S, SWA_KV_HEADS, SWA_HEAD_DIM)

    def band(t):
        pad = jnp.zeros((B, BLOCK) + t.shape[2:], t.dtype)
        tb = jnp.concatenate([pad, t], axis=1).reshape(B, nb + 1, BLOCK, *t.shape[2:])
        tb = jnp.concatenate([tb[:, :-1], tb[:, 1:]], axis=2)
        return jnp.moveaxis(tb, 1, 0)

    return band(k), band(v), band(positions)


def swa_sink(xn, positions, k_band, v_band, pos_band, rel_bias, w_q, b_q, sinks, w_o, b_o):
    B, S, _ = xn.shape
    nb = S // BLOCK
    G = SWA_HEADS // SWA_KV_HEADS
    scale = SWA_HEAD_DIM ** -0.5
    q = (xn @ w_q + b_q).reshape(B, nb, BLOCK, SWA_KV_HEADS, G, SWA_HEAD_DIM)
    q = jnp.moveaxis(q, 1, 0)
    qpos = jnp.moveaxis(positions.reshape(B, nb, BLOCK), 1, 0)
    off = jnp.arange(2 * BLOCK) - BLOCK
    rel_idx = jnp.arange(BLOCK)[:, None] - off[None, :]
    struct_ok = (rel_idx >= 0) & (rel_idx < WINDOW)
    sink = sinks.astype(jnp.float32).reshape(SWA_KV_HEADS, G)
    bias_tab = rel_bias.astype(jnp.float32).reshape(REL_BUCKETS, SWA_KV_HEADS, G)

    def block(args):
        qb, kb, vb, qp, kp, i = args
        s = jnp.einsum('bqhgd,bkhd->bhgqk', qb, kb, preferred_element_type=jnp.float32) * scale
        bucket = t5_bucket(qp[:, :, None] - kp[:, None, :])
        s = s + jnp.transpose(bias_tab[bucket], (0, 3, 4, 1, 2))
        valid = struct_ok & ((i * BLOCK + off) >= 0)[None, :]
        s = jnp.where(valid, s, NEG_INF)
        sink_col = jnp.broadcast_to(sink[None, :, :, None, None], s.shape[:-1] + (1,))
        p = jax.nn.softmax(jnp.concatenate([s, sink_col], axis=-1), axis=-1)[..., :-1]
        return jnp.einsum('bhgqk,bkhd->bqhgd', p.astype(vb.dtype), vb)

    o = lax.map(block, (q, k_band, v_band, qpos, pos_band, jnp.arange(nb)))
    o = jnp.moveaxis(o, 0, 1).reshape(B, S, SWA_HEADS * SWA_HEAD_DIM)
    return o @ w_o + b_o


def swiglu(xn, w_gate, w_up, w_down):
    return (jax.nn.silu(xn @ w_gate) * (xn @ w_up)) @ w_down


def moe(xn, w_router, w_gate, w_up, w_down):
    logits = (xn @ w_router).astype(jnp.float32)
    top_val, top_idx = lax.top_k(logits, TOP_K)
    top_w = jax.nn.softmax(top_val, axis=-1)
    gates = jnp.sum(jax.nn.one_hot(top_idx, N_EXPERTS, dtype=jnp.float32) * top_w[..., None], axis=-2)

    def per_seq(args):
        xs, gs = args
        hg = jnp.einsum('sd,edf->sef', xs, w_gate)
        hu = jnp.einsum('sd,edf->sef', xs, w_up)
        hh = jax.nn.silu(hg) * hu * gs[..., None].astype(xs.dtype)
        return jnp.einsum('sef,efd->sd', hh, w_down)

    return lax.map(per_seq, (xn, gates))


def setup_inputs(seed: int = 0) -> dict:
    key = jax.random.key(seed)
    ks = iter(jax.random.split(key, 40))
    f32 = jnp.float32

    def w(shape, fan_in):
        return jax.random.normal(next(ks), shape, f32) * (fan_in ** -0.5)

    def gain(shape):
        return 1.0 + 0.1 * jax.random.normal(next(ks), shape, f32)

    def small(shape, s=0.02):
        return s * jax.random.normal(next(ks), shape, f32)

    x = jax.random.normal(next(ks), (BATCH, SEQ, D_MODEL), f32)
    offset = jax.random.randint(next(ks), (BATCH, 1), 0, 1024, dtype=jnp.int32)
    positions = (offset + jnp.arange(SEQ, dtype=jnp.int32)[None, :]).astype(jnp.int32)
    kvw = SWA_KV_HEADS * SWA_HEAD_DIM
    return {
        "x": x,
        "positions": positions,
        "mla_w_in": w((N_A_LAYERS, D_MODEL, Q_LORA + KV_LORA + QK_ROPE), D_MODEL),
        "mla_g_q": gain((N_A_LAYERS, Q_LORA)),
        "mla_g_kv": gain((N_A_LAYERS, KV_LORA)),
        "mla_w_uq": w((N_A_LAYERS, Q_LORA, MLA_HEADS * (QK_NOPE + QK_ROPE)), Q_LORA),
        "mla_w_ukv": w((N_A_LAYERS, KV_LORA, MLA_HEADS * (QK_NOPE + V_HEAD)), KV_LORA),
        "mla_w_o": w((N_A_LAYERS, MLA_HEADS * V_HEAD, D_MODEL), MLA_HEADS * V_HEAD),
        "kv_g": gain((D_MODEL,)),
        "kv_w": w((D_MODEL, 2 * kvw), D_MODEL),
        "kv_b": small((2 * kvw,)),
        "swa_w_q": w((N_B_LAYERS, D_MODEL, SWA_HEADS * SWA_HEAD_DIM), D_MODEL),
        "swa_b_q": small((N_B_LAYERS, SWA_HEADS * SWA_HEAD_DIM)),
        "swa_sinks": small((N_B_LAYERS, SWA_HEADS), 0.5),
        "swa_w_o": w((N_B_LAYERS, SWA_HEADS * SWA_HEAD_DIM, D_MODEL), SWA_HEADS * SWA_HEAD_DIM),
        "swa_b_o": small((N_B_LAYERS, D_MODEL)),
        "rel_bias": small((REL_BUCKETS, SWA_HEADS), 0.5),
        "g_attn": gain((DEPTH, D_MODEL)),
        "g_ffn": gain((DEPTH, D_MODEL)),
        "ffn_w_gate": w((N_DENSE, D_MODEL, D_FF), D_MODEL),
        "ffn_w_up": w((N_DENSE, D_MODEL, D_FF), D_MODEL),
        "ffn_w_down": w((N_DENSE, D_FF, D_MODEL), D_FF),
        "moe_w_router": w((N_MOE, D_MODEL, N_EXPERTS), D_MODEL),
        "moe_w_gate": w((N_MOE, N_EXPERTS, D_MODEL, D_FF_EXPERT), D_MODEL),
        "moe_w_up": w((N_MOE, N_EXPERTS, D_MODEL, D_FF_EXPERT), D_MODEL),
        "moe_w_down": w((N_MOE, N_EXPERTS, D_FF_EXPERT, D_MODEL), D_FF_EXPERT),
        "g_final": gain((D_MODEL,)),
    }


def reference(x, positions, mla_w_in, mla_g_q, mla_g_kv, mla_w_uq, mla_w_ukv, mla_w_o,
              kv_g, kv_w, kv_b, swa_w_q, swa_b_q, swa_sinks, swa_w_o, swa_b_o, rel_bias,
              g_attn, g_ffn, ffn_w_gate, ffn_w_up, ffn_w_down,
              moe_w_router, moe_w_gate, moe_w_up, moe_w_down, g_final):
    h = x
    k_band = v_band = pos_band = None
    for l in range(DEPTH):
        xn = rmsnorm(h, g_attn[l])
        if l < N_A_LAYERS:
            a = l
            h = h + mla(xn, positions, mla_w_in[a], mla_g_q[a], mla_g_kv[a],
                        mla_w_uq[a], mla_w_ukv[a], mla_w_o[a])
        else:
            b = l - N_A_LAYERS
            if b == 0:
                k_band, v_band, pos_band = shared_band_kv(h, positions, kv_g, kv_w, kv_b)
            h = h + swa_sink(xn, positions, k_band, v_band, pos_band, rel_bias,
                             swa_w_q[b], swa_b_q[b], swa_sinks[b], swa_w_o[b], swa_b_o[b])
        xn = rmsnorm(h, g_ffn[l])
        if l % 2 == 0:
            j = l // 2
            h = h + swiglu(xn, ffn_w_gate[j], ffn_w_up[j], ffn_w_down[j])
        else:
            j = l // 2
            h = h + moe(xn, moe_w_router[j], moe_w_gate[j], moe_w_up[j], moe_w_down[j])
    return rmsnorm(h, g_final)
```

```python
import functools
import math

import numpy as np
import jax
import jax.numpy as jnp
from jax import lax
from jax.experimental import pallas as pl
from jax.experimental.pallas import tpu as pltpu

MLA_HEADS = 8
QK_NOPE = 128
QK_ROPE = 64
V_HEAD = 128
Q_LORA = 384
KV_LORA = 256
ROPE_THETA = 10000.0
SWA_HEADS = 16
SWA_KV_HEADS = 2
SWA_HEAD_DIM = 64
SWA_GROUPS = SWA_HEADS // SWA_KV_HEADS
WINDOW = 128
BLOCK = 128
REL_BUCKETS = 32
REL_MAX_DIST = 128
N_EXPERTS = 8
EPS = 1e-6
NEG_INF = -1e30

LANE = 128
V7X_VMEM_BYTES = 64 * 1024 * 1024
VMEM_LIMIT = V7X_VMEM_BYTES - 8 * 1024 * 1024

F32 = jnp.float32
BF16 = jnp.bfloat16


def _params(*sem):
    return pltpu.CompilerParams(dimension_semantics=sem, vmem_limit_bytes=VMEM_LIMIT)


def _rms(x, g):
    return x * lax.rsqrt(jnp.mean(x * x, axis=-1, keepdims=True) + EPS) * g


def _dot(a, b):
    return jnp.dot(a, b, preferred_element_type=F32)


def _dot_nt(a, b):
    return lax.dot_general(a, b, (((1,), (1,)), ((), ())), preferred_element_type=F32)


def _rope_table_kernel(pos_ref, inv_ref, cos_ref, sin_ref):
    ang = pos_ref[...].astype(F32) * inv_ref[...]
    cos_ref[...] = jnp.cos(ang)
    sin_ref[...] = jnp.sin(ang)


def rope_tables(pos_col, tm):
    T = pos_col.shape[0]
    half = QK_ROPE // 2
    inv = np.float32(ROPE_THETA) ** (-np.arange(half, dtype=np.float32) / np.float32(half))
    inv128 = jnp.asarray(np.tile(inv.astype(np.float32), LANE // half)[None, :])
    return pl.pallas_call(
        _rope_table_kernel,
        grid=(T // tm,),
        in_specs=[pl.BlockSpec((tm, 1), lambda i: (i, 0)),
                  pl.BlockSpec((1, LANE), lambda i: (0, 0))],
        out_specs=[pl.BlockSpec((tm, LANE), lambda i: (i, 0)),
                   pl.BlockSpec((tm, LANE), lambda i: (i, 0))],
        out_shape=[jax.ShapeDtypeStruct((T, LANE), F32)] * 2,
        compiler_params=_params("arbitrary"),
        name="rope_tables",
    )(pos_col, inv128)


def _mla_in_kernel(x_ref, cos_ref, sin_ref, ga_ref, win_ref, gq_ref, gkv_ref,
                   wq1_ref, wq2_ref, wkv_ref, q_ref, k_ref, v_ref):
    H = MLA_HEADS
    scale = (QK_NOPE + QK_ROPE) ** -0.5
    xn = _rms(x_ref[...], ga_ref[...]).astype(BF16)
    lat = _dot(xn, win_ref[...])
    cq = _rms(lat[:, :Q_LORA], gq_ref[...]).astype(BF16)
    ckv = _rms(lat[:, Q_LORA:Q_LORA + KV_LORA], gkv_ref[...]).astype(BF16)
    cos = cos_ref[...]
    sin = sin_ref[...]
    o = Q_LORA + KV_LORA
    kr = (lat[:, o:o + LANE] * cos + lat[:, o + LANE:o + 2 * LANE] * sin).astype(BF16)
    q1 = _dot(cq, wq1_ref[...])
    q2 = _dot(cq, wq2_ref[...])
    kv = _dot(ckv, wkv_ref[...])
    for h in range(H):
        q_ref[:, 2 * LANE * h:2 * LANE * h + LANE] = (q1[:, 2 * LANE * h:2 * LANE * h + LANE] * scale).astype(BF16)
        qr = q1[:, 2 * LANE * h + LANE:2 * LANE * (h + 1)] * cos + q2[:, LANE * h:LANE * (h + 1)] * sin
        q_ref[:, 2 * LANE * h + LANE:2 * LANE * (h + 1)] = (qr * scale).astype(BF16)
        k_ref[:, 2 * LANE * h:2 * LANE * h + LANE] = kv[:, LANE * h:LANE * (h + 1)].astype(BF16)
        k_ref[:, 2 * LANE * h + LANE:2 * LANE * (h + 1)] = kr
    v_ref[...] = kv[:, H * QK_NOPE:].astype(BF16)


def _mla_weights(w_in, w_uq, w_ukv):
    H, half = MLA_HEADS, QK_ROPE // 2
    D = w_in.shape[0]
    o = Q_LORA + KV_LORA
    wr = w_in[:, o:]
    z = jnp.zeros((D, LANE - QK_ROPE), w_in.dtype)
    wr_sw = jnp.concatenate([-wr[:, half:], wr[:, :half]], axis=1)
    win = jnp.concatenate([w_in[:, :o], wr, z, wr_sw, z], axis=1).astype(BF16)
    wq = w_uq.reshape(Q_LORA, H, QK_NOPE + QK_ROPE)
    qn, qr = wq[:, :, :QK_NOPE], wq[:, :, QK_NOPE:]
    zq = jnp.zeros((Q_LORA, H, LANE - QK_ROPE), w_uq.dtype)
    wq1 = jnp.concatenate([qn, qr, zq], axis=2).reshape(Q_LORA, H * 2 * LANE).astype(BF16)
    qr_sw = jnp.concatenate([-qr[:, :, half:], qr[:, :, :half]], axis=2)
    wq2 = jnp.concatenate([qr_sw, zq], axis=2).reshape(Q_LORA, H * LANE).astype(BF16)
    wkv = w_ukv.reshape(KV_LORA, H, QK_NOPE + V_HEAD)
    wkv = jnp.concatenate([wkv[:, :, :QK_NOPE].reshape(KV_LORA, H * QK_NOPE),
                           wkv[:, :, QK_NOPE:].reshape(KV_LORA, H * V_HEAD)], axis=1).astype(BF16)
    return win, wq1, wq2, wkv


def mla_in(x, cos, sin, g_attn, win, g_q, g_kv, wq1, wq2, wkv, tm):
    T, D = x.shape
    H = MLA_HEADS
    row = lambda i: (i, 0)
    full = lambda i: (0, 0)
    return pl.pallas_call(
        _mla_in_kernel,
        grid=(T // tm,),
        in_specs=[pl.BlockSpec((tm, D), row), pl.BlockSpec((tm, LANE), row), pl.BlockSpec((tm, LANE), row),
                  pl.BlockSpec((1, D), full), pl.BlockSpec(win.shape, full),
                  pl.BlockSpec((1, Q_LORA), full), pl.BlockSpec((1, KV_LORA), full),
                  pl.BlockSpec(wq1.shape, full), pl.BlockSpec(wq2.shape, full), pl.BlockSpec(wkv.shape, full)],
        out_specs=[pl.BlockSpec((tm, H * 2 * LANE), row), pl.BlockSpec((tm, H * 2 * LANE), row),
                   pl.BlockSpec((tm, H * V_HEAD), row)],
        out_shape=[jax.ShapeDtypeStruct((T, H * 2 * LANE), BF16), jax.ShapeDtypeStruct((T, H * 2 * LANE), BF16),
                   jax.ShapeDtypeStruct((T, H * V_HEAD), BF16)],
        compiler_params=_params("arbitrary"),
        name="mla_in",
    )(x, cos, sin, g_attn.reshape(1, D), win, g_q.reshape(1, -1), g_kv.reshape(1, -1), wq1, wq2, wkv)


def _mla_attn_kernel(q_ref, k_ref, v_ref, o_ref, m_ref, l_ref, acc_ref, *, tq):
    i = pl.program_id(2)
    q = q_ref[...]
    m_ref[...] = jnp.full(m_ref.shape, NEG_INF, F32)
    l_ref[...] = jnp.zeros(l_ref.shape, F32)
    acc_ref[...] = jnp.zeros(acc_ref.shape, F32)

    def step(j, masked):
        start = pl.multiple_of(j * tq, tq)
        s = _dot_nt(q, k_ref[pl.ds(start, tq), :])
        if masked:
            r = lax.broadcasted_iota(jnp.int32, s.shape, 0)
            c = lax.broadcasted_iota(jnp.int32, s.shape, 1)
            s = jnp.where(c <= r, s, NEG_INF)
        m_prev = m_ref[...]
        m_new = jnp.maximum(m_prev, jnp.max(s, axis=-1, keepdims=True))
        alpha = jnp.exp(m_prev - m_new)
        p = jnp.exp(s - m_new[:, :1])
        l_ref[...] = alpha * l_ref[...] + jnp.sum(p, axis=-1, keepdims=True)
        acc_ref[...] = alpha * acc_ref[...] + _dot(p.astype(BF16), v_ref[pl.ds(start, tq), :])
        m_ref[...] = m_new

    def body(j, carry):
        step(j, False)
        return carry

    lax.fori_loop(0, i, body, 0)
    step(i, True)
    o_ref[...] = (acc_ref[...] / l_ref[...]).astype(o_ref.dtype)


def mla_attn(q, k, v, B, S, tq):
    H = MLA_HEADS
    nq = S // tq
    return pl.pallas_call(
        functools.partial(_mla_attn_kernel, tq=tq),
        grid=(B, H, nq),
        in_specs=[pl.BlockSpec((tq, 2 * LANE), lambda b, h, i: (b * nq + i, h)),
                  pl.BlockSpec((S, 2 * LANE), lambda b, h, i: (b, h)),
                  pl.BlockSpec((S, V_HEAD), lambda b, h, i: (b, h))],
        out_specs=pl.BlockSpec((tq, V_HEAD), lambda b, h, i: (b * nq + i, h)),
        out_shape=jax.ShapeDtypeStruct((B * S, H * V_HEAD), BF16),
        scratch_shapes=[pltpu.VMEM((tq, V_HEAD), F32)] * 3,
        compiler_params=_params("arbitrary", "arbitrary", "arbitrary"),
        name="mla_attn",
    )(q, k, v)


def _proj_res_kernel(a_ref, w_ref, b_ref, r_ref, o_ref):
    o_ref[...] = r_ref[...] + _dot(a_ref[...], w_ref[...]) + b_ref[...]


def proj_res(a, w, b, res, tm):
    T, K = a.shape
    N = w.shape[1]
    row = lambda i: (i, 0)
    full = lambda i: (0, 0)
    return pl.pallas_call(
        _proj_res_kernel,
        grid=(T // tm,),
        in_specs=[pl.BlockSpec((tm, K), row), pl.BlockSpec((K, N), full), pl.BlockSpec((1, N), full),
                  pl.BlockSpec((tm, N), row)],
        out_specs=pl.BlockSpec((tm, N), row),
        out_shape=jax.ShapeDtypeStruct((T, N), F32),
        compiler_params=_params("arbitrary"),
        name="proj_res",
    )(a, w, b.reshape(1, N), res)


def _norm_proj_kernel(*refs, n, scales):
    x = refs[0][...]
    outs = refs[1 + 3 * n:]
    for p in range(n):
        g_ref, w_ref, b_ref = refs[1 + 3 * p:4 + 3 * p]
        y = _dot(_rms(x, g_ref[...]).astype(BF16), w_ref[...]) + b_ref[...]
        if scales[p] != 1.0:
            y = y * scales[p]
        outs[p][...] = y.astype(outs[p].dtype)


def norm_proj(x, projs, tm):
    T, D = x.shape
    row = lambda i: (i, 0)
    full = lambda i: (0, 0)
    in_specs = [pl.BlockSpec((tm, D), row)]
    args = [x]
    out_specs, out_shape = [], []
    for g, w, b, _ in projs:
        N = w.shape[1]
        in_specs += [pl.BlockSpec((1, D), full), pl.BlockSpec((D, N), full), pl.BlockSpec((1, N), full)]
        args += [g.reshape(1, D), w, b.reshape(1, N)]
        out_specs.append(pl.BlockSpec((tm, N), row))
        out_shape.append(jax.ShapeDtypeStruct((T, N), BF16))
    return pl.pallas_call(
        functools.partial(_norm_proj_kernel, n=len(projs), scales=tuple(float(p[3]) for p in projs)),
        grid=(T // tm,),
        in_specs=in_specs, out_specs=out_specs, out_shape=out_shape,
        compiler_params=_params("arbitrary"),
        name="norm_proj",
    )(*args)


def _t5_bucket(dist):
    n = jnp.maximum(dist, 0)
    max_exact = REL_BUCKETS // 2
    nf = jnp.maximum(n, 1).astype(F32)
    large = max_exact + (jnp.log(nf / max_exact) / math.log(REL_MAX_DIST / max_exact)
                         * (REL_BUCKETS - max_exact)).astype(jnp.int32)
    large = jnp.minimum(large, REL_BUCKETS - 1)
    return jnp.where(n < max_exact, n, large)


def _fill_bias_table(tab_ref, bucket, bias_ref):
    for col in range(SWA_HEADS):
        p, kvh = col // SWA_KV_HEADS, col % SWA_KV_HEADS
        head = kvh * SWA_GROUPS + p
        t = jnp.zeros(bucket.shape, F32)
        for b in range(REL_BUCKETS):
            t = jnp.where(bucket == b, bias_ref[b, head], t)
        tab_ref[col] = t


def _swa_attn_kernel(bias_ref, sink_ref, q_ref, kvp_ref, kvc_ref, qpos_ref, kpp_ref, kpc_ref,
                     o_ref, tab_ref, tab2_ref):
    b_id, i = pl.program_id(0), pl.program_id(1)
    r = lax.broadcasted_iota(jnp.int32, (BLOCK, 2 * BLOCK), 0)
    c = lax.broadcasted_iota(jnp.int32, (BLOCK, 2 * BLOCK), 1)
    rel_idx = r - (c - BLOCK)
    valid = (rel_idx >= 0) & (rel_idx < WINDOW) & ((i * BLOCK + c - BLOCK) >= 0)

    @pl.when((b_id == 0) & (i == 0))
    def _():
        _fill_bias_table(tab_ref, _t5_bucket(rel_idx), bias_ref)

    kpos = jnp.concatenate([kpp_ref[0], kpc_ref[0]], axis=1)
    rel_act = qpos_ref[...] - kpos
    mismatch = jnp.where(valid, jnp.where(rel_act != rel_idx, 1, 0), 0)
    consecutive = jnp.max(mismatch) == 0

    kv = jnp.concatenate([kvp_ref[...], kvc_ref[...]], axis=0)
    lane = lax.broadcasted_iota(jnp.int32, (2 * BLOCK, LANE), 1)
    k_all, v_all = kv[:, :LANE], kv[:, LANE:]
    zero = jnp.zeros_like(k_all)
    ks = [jnp.where(lane < SWA_HEAD_DIM, k_all, zero), jnp.where(lane >= SWA_HEAD_DIM, k_all, zero)]
    vs = [jnp.where(lane < SWA_HEAD_DIM, v_all, zero), jnp.where(lane >= SWA_HEAD_DIM, v_all, zero)]

    def attend(tab):
        for p in range(SWA_GROUPS):
            q2 = q_ref[:, LANE * p:LANE * (p + 1)]
            out = jnp.zeros((BLOCK, LANE), F32)
            for kvh in range(SWA_KV_HEADS):
                col = p * SWA_KV_HEADS + kvh
                s = _dot_nt(q2, ks[kvh]) + tab[col]
                s = jnp.where(valid, s, NEG_INF)
                sink = sink_ref[0, kvh * SWA_GROUPS + p]
                m = jnp.maximum(jnp.max(s, axis=-1, keepdims=True), sink)
                e = jnp.exp(s - m)
                denom = jnp.sum(e, axis=-1, keepdims=True) + jnp.exp(sink - m)
                out = out + _dot((e / denom).astype(BF16), vs[kvh])
            o_ref[:, LANE * p:LANE * (p + 1)] = out.astype(o_ref.dtype)

    @pl.when(consecutive)
    def _():
        attend(tab_ref)

    @pl.when(jnp.logical_not(consecutive))
    def _():
        _fill_bias_table(tab2_ref, _t5_bucket(rel_act), bias_ref)
        attend(tab2_ref)


def swa_attn(q, kv, pos_col, pos_row, rel_bias, sinks, B, S):
    nb = S // BLOCK
    T = B * S
    cur = lambda b, i: (b * nb + i, 0)
    prev = lambda b, i: (b * nb + jnp.maximum(i - 1, 0), 0)
    cur3 = lambda b, i: (b * nb + i, 0, 0)
    prev3 = lambda b, i: (b * nb + jnp.maximum(i - 1, 0), 0, 0)
    smem = pl.BlockSpec(memory_space=pltpu.SMEM)
    return pl.pallas_call(
        _swa_attn_kernel,
        grid=(B, nb),
        in_specs=[smem, smem,
                  pl.BlockSpec((BLOCK, SWA_HEADS * SWA_HEAD_DIM), cur),
                  pl.BlockSpec((BLOCK, 2 * LANE), prev), pl.BlockSpec((BLOCK, 2 * LANE), cur),
                  pl.BlockSpec((BLOCK, 1), cur),
                  pl.BlockSpec((1, 1, BLOCK), prev3), pl.BlockSpec((1, 1, BLOCK), cur3)],
        out_specs=pl.BlockSpec((BLOCK, SWA_HEADS * SWA_HEAD_DIM), cur),
        out_shape=jax.ShapeDtypeStruct((T, SWA_HEADS * SWA_HEAD_DIM), BF16),
        scratch_shapes=[pltpu.VMEM((SWA_HEADS, BLOCK, 2 * BLOCK), F32)] * 2,
        compiler_params=_params("arbitrary", "arbitrary"),
        name="swa_attn",
    )(rel_bias, sinks.reshape(1, -1), q, kv, kv, pos_col, pos_row, pos_row)


def _swa_col_perm():
    idx = []
    for p in range(SWA_GROUPS):
        for kvh in range(SWA_KV_HEADS):
            h = kvh * SWA_GROUPS + p
            idx.extend(range(h * SWA_HEAD_DIM, (h + 1) * SWA_HEAD_DIM))
    return np.asarray(idx, dtype=np.int32)


def _router_gates(xn, wr_ref):
    logits = jnp.dot(xn, wr_ref[...], preferred_element_type=F32, precision=lax.Precision.HIGHEST)
    lane = lax.broadcasted_iota(jnp.int32, logits.shape, 1)
    lg = jnp.where(lane < N_EXPERTS, logits, -jnp.inf)
    m1 = jnp.max(lg, axis=-1, keepdims=True)
    i1 = jnp.min(jnp.where(lg == m1, lane, LANE), axis=-1, keepdims=True)
    lg2 = jnp.where(lane == i1, -jnp.inf, lg)
    m2 = jnp.max(lg2, axis=-1, keepdims=True)
    i2 = jnp.min(jnp.where(lg2 == m2, lane, LANE), axis=-1, keepdims=True)
    e2 = jnp.exp(m2 - m1)
    w1 = 1.0 / (1.0 + e2)
    return jnp.where(lane == i1, w1, 0.0) + jnp.where(lane == i2, e2 * w1, 0.0)


def _ffn_kernel(*refs, n_e, n_f, moe, final):
    it = iter(refs)
    x_ref, g_ref = next(it), next(it)
    wr_ref = next(it) if moe else None
    wg_ref, wu_ref, wd_ref = next(it), next(it), next(it)
    gf_ref = next(it) if final else None
    o_ref, xn_ref, acc_ref = next(it), next(it), next(it)
    gates_ref = next(it) if moe else None
    e, f = pl.program_id(1), pl.program_id(2)

    @pl.when((e == 0) & (f == 0))
    def _():
        xn = _rms(x_ref[...], g_ref[...])
        xn_ref[...] = xn.astype(BF16)
        acc_ref[...] = jnp.zeros(acc_ref.shape, F32)
        if moe:
            gates_ref[...] = _router_gates(xn, wr_ref)

    xn = xn_ref[...]
    hg = _dot(xn, wg_ref[0])
    hu = _dot(xn, wu_ref[0])
    hh = hg * jax.nn.sigmoid(hg) * hu
    if moe:
        lane = lax.broadcasted_iota(jnp.int32, gates_ref.shape, 1)
        hh = hh * jnp.sum(jnp.where(lane == e, gates_ref[...], 0.0), axis=-1, keepdims=True)
    acc_ref[...] += _dot(hh.astype(BF16), wd_ref[0])

    @pl.when((e == n_e - 1) & (f == n_f - 1))
    def _():
        y = x_ref[...] + acc_ref[...]
        if final:
            y = _rms(y, gf_ref[...])
        o_ref[...] = y


def ffn(x, g, wg, wu, wd, tm, tf, w_router=None, g_final=None):
    T, D = x.shape
    E, _, Fd = wg.shape
    n_f = Fd // tf
    moe, final = w_router is not None, g_final is not None
    row = lambda i, e, f: (i, 0)
    full = lambda i, e, f: (0, 0)
    in_specs = [pl.BlockSpec((tm, D), row), pl.BlockSpec((1, D), full)]
    args = [x, g.reshape(1, D)]
    if moe:
        in_specs.append(pl.BlockSpec((D, LANE), full))
        args.append(w_router)
    in_specs += [pl.BlockSpec((1, D, tf), lambda i, e, f: (e, 0, f)),
                 pl.BlockSpec((1, D, tf), lambda i, e, f: (e, 0, f)),
                 pl.BlockSpec((1, tf, D), lambda i, e, f: (e, f, 0))]
    args += [wg, wu, wd]
    if final:
        in_specs.append(pl.BlockSpec((1, D), full))
        args.append(g_final.reshape(1, D))
    scratch = [pltpu.VMEM((tm, D), BF16), pltpu.VMEM((tm, D), F32)]
    if moe:
        scratch.append(pltpu.VMEM((tm, LANE), F32))
    return pl.pallas_call(
        functools.partial(_ffn_kernel, n_e=E, n_f=n_f, moe=moe, final=final),
        grid=(T // tm, E, n_f),
        in_specs=in_specs,
        out_specs=pl.BlockSpec((tm, D), row),
        out_shape=jax.ShapeDtypeStruct((T, D), F32),
        scratch_shapes=scratch,
        compiler_params=_params("arbitrary", "arbitrary", "arbitrary"),
        name="moe_ffn" if moe else "dense_ffn",
    )(*args)


def kernel(x, positions, mla_w_in, mla_g_q, mla_g_kv, mla_w_uq, mla_w_ukv, mla_w_o, kv_g, kv_w, kv_b, swa_w_q, swa_b_q, swa_sinks, swa_w_o, swa_b_o, rel_bias, g_attn, g_ffn, ffn_w_gate, ffn_w_up, ffn_w_down, moe_w_router, moe_w_gate, moe_w_up, moe_w_down, g_final):
    B, S, D = x.shape
    T = B * S
    depth = g_attn.shape[0]
    n_a = mla_w_in.shape[0]
    tm_proj = min(512, T)
    tm_ffn = min(1024, T)
    tq = min(256, S)

    h = x.reshape(T, D)
    pos_col = positions.reshape(T, 1)
    pos_row = positions.reshape(T // BLOCK, 1, BLOCK)
    cos, sin = rope_tables(pos_col, min(1024, T))
    perm = _swa_col_perm()
    zeros_d = jnp.zeros((D,), F32)
    kvw = SWA_KV_HEADS * SWA_HEAD_DIM
    kv = None

    for l in range(depth):
        if l < n_a:
            win, wq1, wq2, wkv = _mla_weights(mla_w_in[l], mla_w_uq[l], mla_w_ukv[l])
            q, k, v = mla_in(h, cos, sin, g_attn[l], win, mla_g_q[l], mla_g_kv[l], wq1, wq2, wkv, tm_proj)
            o = mla_attn(q, k, v, B, S, tq)
            h = proj_res(o, mla_w_o[l].astype(BF16), zeros_d, h, tm_ffn)
        else:
            b = l - n_a
            qproj = (g_attn[l], swa_w_q[b][:, perm].astype(BF16), swa_b_q[b][perm], SWA_HEAD_DIM ** -0.5)
            if b == 0:
                q, kv = norm_proj(h, [qproj, (kv_g, kv_w.astype(BF16), kv_b, 1.0)], tm_proj)
            else:
                (q,) = norm_proj(h, [qproj], tm_proj)
            o = swa_attn(q, kv, pos_col, pos_row, rel_bias, swa_sinks[b], B, S)
            h = proj_res(o, swa_w_o[b][perm, :].astype(BF16), swa_b_o[b], h, tm_ffn)
        j = l // 2
        gf = g_final if l == depth - 1 else None
        if l % 2 == 0:
            h = ffn(h, g_ffn[l], ffn_w_gate[j][None].astype(BF16), ffn_w_up[j][None].astype(BF16),
                    ffn_w_down[j][None].astype(BF16), tm_ffn, 256, g_final=gf)
        else:
            wr = jnp.pad(moe_w_router[j], ((0, 0), (0, LANE - N_EXPERTS)))
            h = ffn(h, g_ffn[l], moe_w_gate[j].astype(BF16), moe_w_up[j].astype(BF16),
                    moe_w_down[j].astype(BF16), tm_ffn, 512, w_router=wr, g_final=gf)
    return h.reshape(B, S, D)
```

```python
import functools
import math

import numpy as np
import jax
import jax.numpy as jnp
from jax import lax
from jax.experimental import pallas as pl
from jax.experimental.pallas import tpu as pltpu

MLA_HEADS = 8
QK_NOPE = 128
QK_ROPE = 64
V_HEAD = 128
Q_LORA = 384
KV_LORA = 256
ROPE_THETA = 10000.0
SWA_HEADS = 16
SWA_KV_HEADS = 2
SWA_HEAD_DIM = 64
SWA_GROUPS = SWA_HEADS // SWA_KV_HEADS
WINDOW = 128
BLOCK = 128
REL_BUCKETS = 32
REL_MAX_DIST = 128
N_EXPERTS = 8
EPS = 1e-6
NEG_INF = -1e30
LOG2E = math.log2(math.e)

LANE = 128
V7X_VMEM_BYTES = 64 * 1024 * 1024
VMEM_LIMIT = V7X_VMEM_BYTES - 8 * 1024 * 1024

F32 = jnp.float32
BF16 = jnp.bfloat16


def _params(*sem):
    return pltpu.CompilerParams(dimension_semantics=sem, vmem_limit_bytes=VMEM_LIMIT)


def _rms(x, g):
    return x * lax.rsqrt(jnp.mean(x * x, axis=-1, keepdims=True) + EPS) * g


def _dot(a, b):
    return jnp.dot(a, b, preferred_element_type=F32)


def _dot_nt(a, b):
    return lax.dot_general(a, b, (((1,), (1,)), ((), ())), preferred_element_type=F32)


def _rope_table_kernel(pos_ref, inv_ref, cos_ref, sin_ref):
    ang = pos_ref[...].astype(F32) * inv_ref[...]
    cos_ref[...] = jnp.cos(ang)
    sin_ref[...] = jnp.sin(ang)


def rope_tables(pos_col, tm):
    T = pos_col.shape[0]
    half = QK_ROPE // 2
    inv = np.float32(ROPE_THETA) ** (-np.arange(half, dtype=np.float32) / np.float32(half))
    inv128 = jnp.asarray(np.tile(inv.astype(np.float32), LANE // half)[None, :])
    return pl.pallas_call(
        _rope_table_kernel,
        grid=(T // tm,),
        in_specs=[pl.BlockSpec((tm, 1), lambda i: (i, 0)),
                  pl.BlockSpec((1, LANE), lambda i: (0, 0))],
        out_specs=[pl.BlockSpec((tm, LANE), lambda i: (i, 0)),
                   pl.BlockSpec((tm, LANE), lambda i: (i, 0))],
        out_shape=[jax.ShapeDtypeStruct((T, LANE), F32)] * 2,
        compiler_params=_params("arbitrary"),
        name="rope_tables",
    )(pos_col, inv128)


def _mla_in_kernel(x_ref, cos_ref, sin_ref, ga_ref, win_ref, gq_ref, gkv_ref,
                   wq1_ref, wq2_ref, wkv_ref, q_ref, k_ref, v_ref):
    H = MLA_HEADS
    scale = (QK_NOPE + QK_ROPE) ** -0.5 * LOG2E
    xn = _rms(x_ref[...], ga_ref[...]).astype(BF16)
    lat = _dot(xn, win_ref[...])
    cq = _rms(lat[:, :Q_LORA], gq_ref[...]).astype(BF16)
    ckv = _rms(lat[:, Q_LORA:Q_LORA + KV_LORA], gkv_ref[...]).astype(BF16)
    cos = cos_ref[...]
    sin = sin_ref[...]
    o = Q_LORA + KV_LORA
    kr = (lat[:, o:o + LANE] * cos + lat[:, o + LANE:o + 2 * LANE] * sin).astype(BF16)
    q1 = _dot(cq, wq1_ref[...])
    q2 = _dot(cq, wq2_ref[...])
    kv = _dot(ckv, wkv_ref[...])
    for h in range(H):
        q_ref[:, 2 * LANE * h:2 * LANE * h + LANE] = (q1[:, 2 * LANE * h:2 * LANE * h + LANE] * scale).astype(BF16)
        qr = q1[:, 2 * LANE * h + LANE:2 * LANE * (h + 1)] * cos + q2[:, LANE * h:LANE * (h + 1)] * sin
        q_ref[:, 2 * LANE * h + LANE:2 * LANE * (h + 1)] = (qr * scale).astype(BF16)
        k_ref[:, 2 * LANE * h:2 * LANE * h + LANE] = kv[:, LANE * h:LANE * (h + 1)].astype(BF16)
        k_ref[:, 2 * LANE * h + LANE:2 * LANE * (h + 1)] = kr
    v_ref[...] = kv[:, H * QK_NOPE:].astype(BF16)


def _mla_weights(w_in, w_uq, w_ukv):
    H, half = MLA_HEADS, QK_ROPE // 2
    D = w_in.shape[0]
    o = Q_LORA + KV_LORA
    wr = w_in[:, o:]
    z = jnp.zeros((D, LANE - QK_ROPE), w_in.dtype)
    wr_sw = jnp.concatenate([-wr[:, half:], wr[:, :half]], axis=1)
    win = jnp.concatenate([w_in[:, :o], wr, z, wr_sw, z], axis=1).astype(BF16)
    wq = w_uq.reshape(Q_LORA, H, QK_NOPE + QK_ROPE)
    qn, qr = wq[:, :, :QK_NOPE], wq[:, :, QK_NOPE:]
    zq = jnp.zeros((Q_LORA, H, LANE - QK_ROPE), w_uq.dtype)
    wq1 = jnp.concatenate([qn, qr, zq], axis=2).reshape(Q_LORA, H * 2 * LANE).astype(BF16)
    qr_sw = jnp.concatenate([-qr[:, :, half:], qr[:, :, :half]], axis=2)
    wq2 = jnp.concatenate([qr_sw, zq], axis=2).reshape(Q_LORA, H * LANE).astype(BF16)
    wkv = w_ukv.reshape(KV_LORA, H, QK_NOPE + V_HEAD)
    wkv = jnp.concatenate([wkv[:, :, :QK_NOPE].reshape(KV_LORA, H * QK_NOPE),
                           wkv[:, :, QK_NOPE:].reshape(KV_LORA, H * V_HEAD)], axis=1).astype(BF16)
    return win, wq1, wq2, wkv


def mla_in(x, cos, sin, g_attn, win, g_q, g_kv, wq1, wq2, wkv, tm):
    T, D = x.shape
    H = MLA_HEADS
    row = lambda i: (i, 0)
    full = lambda i: (0, 0)
    return pl.pallas_call(
        _mla_in_kernel,
        grid=(T // tm,),
        in_specs=[pl.BlockSpec((tm, D), row), pl.BlockSpec((tm, LANE), row), pl.BlockSpec((tm, LANE), row),
                  pl.BlockSpec((1, D), full), pl.BlockSpec(win.shape, full),
                  pl.BlockSpec((1, Q_LORA), full), pl.BlockSpec((1, KV_LORA), full),
                  pl.BlockSpec(wq1.shape, full), pl.BlockSpec(wq2.shape, full), pl.BlockSpec(wkv.shape, full)],
        out_specs=[pl.BlockSpec((tm, H * 2 * LANE), row), pl.BlockSpec((tm, H * 2 * LANE), row),
                   pl.BlockSpec((tm, H * V_HEAD), row)],
        out_shape=[jax.ShapeDtypeStruct((T, H * 2 * LANE), BF16), jax.ShapeDtypeStruct((T, H * 2 * LANE), BF16),
                   jax.ShapeDtypeStruct((T, H * V_HEAD), BF16)],
        compiler_params=_params("arbitrary"),
        name="mla_in",
    )(x, cos, sin, g_attn.reshape(1, D), win, g_q.reshape(1, -1), g_kv.reshape(1, -1), wq1, wq2, wkv)


def _mla_attn_kernel(q_ref, k_ref, v_ref, o_ref, *, tq, nq):
    i = pl.program_id(2)
    q = q_ref[...]
    r = lax.broadcasted_iota(jnp.int32, (tq, tq), 0)
    c = lax.broadcasted_iota(jnp.int32, (tq, tq), 1)

    def prefix(n_off):
        lo = n_off * tq
        s_d = jnp.where(c <= r, _dot_nt(q, k_ref[lo:lo + tq, :]), NEG_INF)
        m = jnp.max(s_d, axis=-1, keepdims=True)
        if n_off:
            s_o = _dot_nt(q, k_ref[:lo, :])
            m = jnp.maximum(m, jnp.max(s_o, axis=-1, keepdims=True))
            p_o = jnp.exp2(s_o - m)
        p_d = jnp.exp2(s_d - m)
        l = jnp.sum(p_d, axis=-1, keepdims=True)
        acc = _dot(p_d.astype(BF16), v_ref[lo:lo + tq, :])
        if n_off:
            l = l + jnp.sum(p_o, axis=-1, keepdims=True)
            acc = acc + _dot(p_o.astype(BF16), v_ref[:lo, :])
        o_ref[...] = (acc * (1.0 / l)).astype(o_ref.dtype)

    for n_off in range(nq):
        pl.when(i == n_off)(functools.partial(prefix, n_off))


def mla_attn(q, k, v, B, S, tq):
    H = MLA_HEADS
    nq = S // tq
    return pl.pallas_call(
        functools.partial(_mla_attn_kernel, tq=tq, nq=nq),
        grid=(B, H, nq),
        in_specs=[pl.BlockSpec((tq, 2 * LANE), lambda b, h, i: (b * nq + i, h)),
                  pl.BlockSpec((S, 2 * LANE), lambda b, h, i: (b, h)),
                  pl.BlockSpec((S, V_HEAD), lambda b, h, i: (b, h))],
        out_specs=pl.BlockSpec((tq, V_HEAD), lambda b, h, i: (b * nq + i, h)),
        out_shape=jax.ShapeDtypeStruct((B * S, H * V_HEAD), BF16),
        compiler_params=_params("arbitrary", "arbitrary", "arbitrary"),
        name="mla_attn",
    )(q, k, v)


def _proj_res_kernel(a_ref, w_ref, b_ref, r_ref, o_ref):
    o_ref[...] = r_ref[...] + _dot(a_ref[...], w_ref[...]) + b_ref[...]


def proj_res(a, w, b, res, tm):
    T, K = a.shape
    N = w.shape[1]
    row = lambda i: (i, 0)
    full = lambda i: (0, 0)
    return pl.pallas_call(
        _proj_res_kernel,
        grid=(T // tm,),
        in_specs=[pl.BlockSpec((tm, K), row), pl.BlockSpec((K, N), full), pl.BlockSpec((1, N), full),
                  pl.BlockSpec((tm, N), row)],
        out_specs=pl.BlockSpec((tm, N), row),
        out_shape=jax.ShapeDtypeStruct((T, N), F32),
        compiler_params=_params("arbitrary"),
        name="proj_res",
    )(a, w, b.reshape(1, N), res)


def _norm_proj_kernel(*refs, n, scales):
    x = refs[0][...]
    outs = refs[1 + 3 * n:]
    for p in range(n):
        g_ref, w_ref, b_ref = refs[1 + 3 * p:4 + 3 * p]
        y = _dot(_rms(x, g_ref[...]).astype(BF16), w_ref[...]) + b_ref[...]
        if scales[p] != 1.0:
            y = y * scales[p]
        outs[p][...] = y.astype(outs[p].dtype)


def norm_proj(x, projs, tm):
    T, D = x.shape
    row = lambda i: (i, 0)
    full = lambda i: (0, 0)
    in_specs = [pl.BlockSpec((tm, D), row)]
    args = [x]
    out_specs, out_shape = [], []
    for g, w, b, _ in projs:
        N = w.shape[1]
        in_specs += [pl.BlockSpec((1, D), full), pl.BlockSpec((D, N), full), pl.BlockSpec((1, N), full)]
        args += [g.reshape(1, D), w, b.reshape(1, N)]
        out_specs.append(pl.BlockSpec((tm, N), row))
        out_shape.append(jax.ShapeDtypeStruct((T, N), BF16))
    return pl.pallas_call(
        functools.partial(_norm_proj_kernel, n=len(projs), scales=tuple(float(p[3]) for p in projs)),
        grid=(T // tm,),
        in_specs=in_specs, out_specs=out_specs, out_shape=out_shape,
        compiler_params=_params("arbitrary"),
        name="norm_proj",
    )(*args)


def _t5_bucket(dist):
    n = jnp.maximum(dist, 0)
    max_exact = REL_BUCKETS // 2
    nf = jnp.maximum(n, 1).astype(F32)
    large = max_exact + (jnp.log(nf / max_exact) / math.log(REL_MAX_DIST / max_exact)
                         * (REL_BUCKETS - max_exact)).astype(jnp.int32)
    large = jnp.minimum(large, REL_BUCKETS - 1)
    return jnp.where(n < max_exact, n, large)


def _fill_swa_table(tab_ref, slot, bucket, valid, col0, bias_ref, sink_ref):
    for kvh in range(SWA_KV_HEADS):
        def body(g, carry, kvh=kvh):
            head = kvh * SWA_GROUPS + g
            t = jnp.zeros(bucket.shape, F32)
            for b in range(REL_BUCKETS):
                t = jnp.where(bucket == b, bias_ref[b, head], t)
            t = jnp.where(valid, t * LOG2E, NEG_INF)
            t = jnp.where(col0, sink_ref[0, head] * LOG2E, t)
            tab_ref[slot, kvh, pl.ds(pl.multiple_of(g * BLOCK, BLOCK), BLOCK), :] = t
            return carry
        lax.fori_loop(0, SWA_GROUPS, body, 0)


def _swa_attn_kernel(bias_ref, sink_ref, q_ref, kvp_ref, kvc_ref, qpos_ref, kpp_ref, kpc_ref,
                     o_ref, tab_ref):
    b_id, i = pl.program_id(0), pl.program_id(1)
    r = lax.broadcasted_iota(jnp.int32, (BLOCK, 2 * BLOCK), 0)
    c = lax.broadcasted_iota(jnp.int32, (BLOCK, 2 * BLOCK), 1)
    rel_idx = r - (c - BLOCK)
    in_window = (rel_idx >= 0) & (rel_idx < WINDOW)
    in_seq = in_window & (c >= BLOCK)
    valid = in_window & ((i * BLOCK + c - BLOCK) >= 0)
    col0 = c == 0

    @pl.when((b_id == 0) & (i == 0))
    def _():
        bucket = _t5_bucket(rel_idx)
        _fill_swa_table(tab_ref, 0, bucket, in_seq, col0, bias_ref, sink_ref)
        _fill_swa_table(tab_ref, 1, bucket, in_window, col0, bias_ref, sink_ref)

    kpos = jnp.concatenate([kpp_ref[0], kpc_ref[0]], axis=1)
    rel_act = qpos_ref[...] - kpos
    mismatch = jnp.where(valid, jnp.where(rel_act != rel_idx, 1, 0), 0)
    consecutive = jnp.max(mismatch) == 0

    @pl.when(jnp.logical_not(consecutive))
    def _():
        _fill_swa_table(tab_ref, 2, _t5_bucket(rel_act), valid, col0, bias_ref, sink_ref)

    slot = jnp.where(consecutive, jnp.where(i == 0, 0, 1), 2)

    kv = jnp.concatenate([kvp_ref[...], kvc_ref[...]], axis=0)
    lane = lax.broadcasted_iota(jnp.int32, (2 * BLOCK, LANE), 1)
    key = lax.broadcasted_iota(jnp.int32, (2 * BLOCK, LANE), 0)
    k_all, v_all = kv[:, :LANE], kv[:, LANE:]
    zero = jnp.zeros_like(k_all)
    q_stack = jnp.concatenate([q_ref[:, LANE * g:LANE * (g + 1)] for g in range(SWA_GROUPS)], axis=0)
    out = jnp.zeros((SWA_GROUPS * BLOCK, LANE), F32)
    for kvh in range(SWA_KV_HEADS):
        mine = (lane < SWA_HEAD_DIM) if kvh == 0 else (lane >= SWA_HEAD_DIM)
        k_h = jnp.where(mine, k_all, zero)
        v_h = jnp.where(mine & (key > 0), v_all, zero)
        s = _dot_nt(q_stack, k_h) + tab_ref[slot, kvh]
        e = jnp.exp2(s - jnp.max(s, axis=-1, keepdims=True))
        denom = jnp.sum(e, axis=-1, keepdims=True)
        out = out + _dot(e.astype(BF16), v_h) * (1.0 / denom)
    for g in range(SWA_GROUPS):
        o_ref[:, LANE * g:LANE * (g + 1)] = out[g * BLOCK:(g + 1) * BLOCK, :].astype(o_ref.dtype)


def swa_attn(q, kv, pos_col, pos_row, rel_bias, sinks, B, S):
    nb = S // BLOCK
    T = B * S
    cur = lambda b, i: (b * nb + i, 0)
    prev = lambda b, i: (b * nb + jnp.maximum(i - 1, 0), 0)
    cur3 = lambda b, i: (b * nb + i, 0, 0)
    prev3 = lambda b, i: (b * nb + jnp.maximum(i - 1, 0), 0, 0)
    smem = pl.BlockSpec(memory_space=pltpu.SMEM)
    return pl.pallas_call(
        _swa_attn_kernel,
        grid=(B, nb),
        in_specs=[smem, smem,
                  pl.BlockSpec((BLOCK, SWA_HEADS * SWA_HEAD_DIM), cur),
                  pl.BlockSpec((BLOCK, 2 * LANE), prev), pl.BlockSpec((BLOCK, 2 * LANE), cur),
                  pl.BlockSpec((BLOCK, 1), cur),
                  pl.BlockSpec((1, 1, BLOCK), prev3), pl.BlockSpec((1, 1, BLOCK), cur3)],
        out_specs=pl.BlockSpec((BLOCK, SWA_HEADS * SWA_HEAD_DIM), cur),
        out_shape=jax.ShapeDtypeStruct((T, SWA_HEADS * SWA_HEAD_DIM), BF16),
        scratch_shapes=[pltpu.VMEM((3, SWA_KV_HEADS, SWA_GROUPS * BLOCK, 2 * BLOCK), F32)],
        compiler_params=_params("arbitrary", "arbitrary"),
        name="swa_attn",
    )(rel_bias, sinks.reshape(1, -1), q, kv, kv, pos_col, pos_row, pos_row)


def _swa_col_perm():
    idx = []
    for p in range(SWA_GROUPS):
        for kvh in range(SWA_KV_HEADS):
            h = kvh * SWA_GROUPS + p
            idx.extend(range(h * SWA_HEAD_DIM, (h + 1) * SWA_HEAD_DIM))
    return np.asarray(idx, dtype=np.int32)


def _ffn_kernel(*refs, n_f, final):
    it = iter(refs)
    x_ref, g_ref, wg_ref, wu_ref, wd_ref = next(it), next(it), next(it), next(it), next(it)
    gf_ref = next(it) if final else None
    o_ref, xn_ref, acc_ref = next(it), next(it), next(it)
    f = pl.program_id(1)

    @pl.when(f == 0)
    def _():
        xn_ref[...] = _rms(x_ref[...], g_ref[...]).astype(BF16)
        acc_ref[...] = jnp.zeros(acc_ref.shape, F32)

    xn = xn_ref[...]
    hg = _dot(xn, wg_ref[...])
    hu = _dot(xn, wu_ref[...])
    acc_ref[...] += _dot((hg * jax.nn.sigmoid(hg) * hu).astype(BF16), wd_ref[...])

    @pl.when(f == n_f - 1)
    def _():
        y = x_ref[...] + acc_ref[...]
        if final:
            y = _rms(y, gf_ref[...])
        o_ref[...] = y


def ffn(x, g, wg, wu, wd, tm, tf, g_final=None):
    T, D = x.shape
    Fd = wg.shape[1]
    n_f = Fd // tf
    final = g_final is not None
    row = lambda i, f: (i, 0)
    full = lambda i, f: (0, 0)
    in_specs = [pl.BlockSpec((tm, D), row), pl.BlockSpec((1, D), full),
                pl.BlockSpec((D, tf), lambda i, f: (0, f)), pl.BlockSpec((D, tf), lambda i, f: (0, f)),
                pl.BlockSpec((tf, D), lambda i, f: (f, 0))]
    args = [x, g.reshape(1, D), wg, wu, wd]
    if final:
        in_specs.append(pl.BlockSpec((1, D), full))
        args.append(g_final.reshape(1, D))
    return pl.pallas_call(
        functools.partial(_ffn_kernel, n_f=n_f, final=final),
        grid=(T // tm, n_f),
        in_specs=in_specs,
        out_specs=pl.BlockSpec((tm, D), row),
        out_shape=jax.ShapeDtypeStruct((T, D), F32),
        scratch_shapes=[pltpu.VMEM((tm, D), BF16), pltpu.VMEM((tm, D), F32)],
        compiler_params=_params("arbitrary", "arbitrary"),
        name="dense_ffn",
    )(*args)


def _moe_route_kernel(x_ref, g_ref, wr_ref, info_ref):
    xn = _rms(x_ref[...], g_ref[...])
    logits = jnp.dot(xn, wr_ref[...], preferred_element_type=F32, precision=lax.Precision.HIGHEST)
    lane = lax.broadcasted_iota(jnp.int32, logits.shape, 1)
    lg = jnp.where(lane < N_EXPERTS, logits, -jnp.inf)
    m1 = jnp.max(lg, axis=-1, keepdims=True)
    i1 = jnp.min(jnp.where(lg == m1, lane, LANE), axis=-1, keepdims=True)
    lg2 = jnp.where(lane == i1, -jnp.inf, lg)
    m2 = jnp.max(lg2, axis=-1, keepdims=True)
    i2 = jnp.min(jnp.where(lg2 == m2, lane, LANE), axis=-1, keepdims=True)
    e2 = jnp.exp(m2 - m1)
    w1 = 1.0 / (1.0 + e2)
    info = jnp.where(lane == 0, i1.astype(F32), 0.0)
    info = jnp.where(lane == 1, i2.astype(F32), info)
    info = jnp.where(lane == 2, w1, info)
    info_ref[...] = jnp.where(lane == 3, e2 * w1, info)


def moe_route(x, g, wr, tm):
    T, D = x.shape
    row = lambda i: (i, 0)
    full = lambda i: (0, 0)
    return pl.pallas_call(
        _moe_route_kernel,
        grid=(T // tm,),
        in_specs=[pl.BlockSpec((tm, D), row), pl.BlockSpec((1, D), full), pl.BlockSpec((D, LANE), full)],
        out_specs=pl.BlockSpec((tm, LANE), row),
        out_shape=jax.ShapeDtypeStruct((T, LANE), F32),
        compiler_params=_params("arbitrary"),
        name="moe_route",
    )(x, g.reshape(1, D), wr)


def _row_copies(src_ref, dst_ref, sem, n, src_row, dst_row):
    def copy(t):
        return pltpu.make_async_copy(src_ref.at[pl.ds(src_row(t), 1)], dst_ref.at[pl.ds(dst_row(t), 1)], sem)

    def start(t, carry):
        copy(t).start()
        return carry

    def wait(t, carry):
        copy(t).wait()
        return carry

    lax.fori_loop(0, n, start, 0, unroll=8)
    lax.fori_loop(0, n, wait, 0, unroll=8)


def _moe_dispatch_kernel(s1_ref, s2_ref, x_ref, g_ref, xs_in_ref, xs_ref, xn_ref, sem):
    del xs_in_ref
    tm = xn_ref.shape[0]
    xn_ref[...] = _rms(x_ref[...], g_ref[...])
    for s_ref in (s1_ref, s2_ref):
        _row_copies(xn_ref, xs_ref, sem, tm, lambda t: t, lambda t, s_ref=s_ref: s_ref[0, 0, t])


def moe_dispatch(x, g, slot1, slot2, n_slots, tm):
    T, D = x.shape
    row = lambda i: (i, 0)
    full = lambda i: (0, 0)
    slots = pl.BlockSpec((1, 1, tm), lambda i: (i, 0, 0), memory_space=pltpu.SMEM)
    return pl.pallas_call(
        _moe_dispatch_kernel,
        grid=(T // tm,),
        in_specs=[slots, slots, pl.BlockSpec((tm, D), row), pl.BlockSpec((1, D), full),
                  pl.BlockSpec(memory_space=pl.ANY)],
        out_specs=pl.BlockSpec(memory_space=pl.ANY),
        out_shape=jax.ShapeDtypeStruct((n_slots, D), F32),
        scratch_shapes=[pltpu.VMEM((tm, D), F32), pltpu.SemaphoreType.DMA(())],
        input_output_aliases={4: 0},
        compiler_params=_params("arbitrary"),
        name="moe_dispatch",
    )(slot1.reshape(T // tm, 1, tm), slot2.reshape(T // tm, 1, tm), x, g.reshape(1, D),
      jnp.zeros((n_slots, D), F32))


def _moe_expert_kernel(src_ref, exp_ref, act_ref, x_ref, wg_ref, wu_ref, wd_ref, y_ref, xb_ref):
    del src_ref, exp_ref
    j, f = pl.program_id(0), pl.program_id(1)

    @pl.when(act_ref[j] == 1)
    def _():
        @pl.when(f == 0)
        def _():
            xb_ref[...] = x_ref[...].astype(BF16)

        xb = xb_ref[...]
        hg = _dot(xb, wg_ref[0])
        hu = _dot(xb, wu_ref[0])
        part = _dot((hg * jax.nn.sigmoid(hg) * hu).astype(BF16), wd_ref[0])

        @pl.when(f == 0)
        def _():
            y_ref[...] = part

        @pl.when(f > 0)
        def _():
            y_ref[...] += part

    @pl.when((act_ref[j] == 0) & (f == 0))
    def _():
        y_ref[...] = jnp.zeros(y_ref.shape, F32)


def moe_experts(xs, wg, wu, wd, tile_src, tile_expert, tile_active, te, tf):
    n_slots, D = xs.shape
    Fd = wg.shape[2]
    n_f = Fd // tf
    n_tiles = n_slots // te

    def fidx(j, f, act):
        return jnp.where(act[j] == 1, f, n_f - 1)

    grid_spec = pltpu.PrefetchScalarGridSpec(
        num_scalar_prefetch=3,
        grid=(n_tiles, n_f),
        in_specs=[pl.BlockSpec((te, D), lambda j, f, src, exp, act: (src[j], 0)),
                  pl.BlockSpec((1, D, tf), lambda j, f, src, exp, act: (exp[j], 0, fidx(j, f, act))),
                  pl.BlockSpec((1, D, tf), lambda j, f, src, exp, act: (exp[j], 0, fidx(j, f, act))),
                  pl.BlockSpec((1, tf, D), lambda j, f, src, exp, act: (exp[j], fidx(j, f, act), 0))],
        out_specs=pl.BlockSpec((te, D), lambda j, f, src, exp, act: (j, 0)),
        scratch_shapes=[pltpu.VMEM((te, D), BF16)],
    )
    return pl.pallas_call(
        _moe_expert_kernel,
        grid_spec=grid_spec,
        out_shape=jax.ShapeDtypeStruct((n_slots, D), F32),
        compiler_params=_params("arbitrary", "arbitrary"),
        name="moe_experts",
    )(tile_src, tile_expert, tile_active, xs, wg, wu, wd)


def _moe_combine_kernel(*refs, final):
    it = iter(refs)
    s1_ref, s2_ref, x_ref, info_ref = next(it), next(it), next(it), next(it)
    gf_ref = next(it) if final else None
    y_ref, o_ref, b1_ref, b2_ref, sem = next(it), next(it), next(it), next(it), next(it)
    tm = b1_ref.shape[0]
    for s_ref, b_ref in ((s1_ref, b1_ref), (s2_ref, b2_ref)):
        _row_copies(y_ref, b_ref, sem, tm, lambda t, s_ref=s_ref: s_ref[0, 0, t], lambda t: t)
    info = info_ref[...]
    y = x_ref[...] + info[:, 2:3] * b1_ref[...] + info[:, 3:4] * b2_ref[...]
    if final:
        y = _rms(y, gf_ref[...])
    o_ref[...] = y


def moe_combine(x, info, slot1, slot2, y, tm, g_final=None):
    T, D = x.shape
    final = g_final is not None
    row = lambda i: (i, 0)
    full = lambda i: (0, 0)
    slots = pl.BlockSpec((1, 1, tm), lambda i: (i, 0, 0), memory_space=pltpu.SMEM)
    in_specs = [slots, slots, pl.BlockSpec((tm, D), row), pl.BlockSpec((tm, LANE), row)]
    args = [slot1.reshape(T // tm, 1, tm), slot2.reshape(T // tm, 1, tm), x, info]
    if final:
        in_specs.append(pl.BlockSpec((1, D), full))
        args.append(g_final.reshape(1, D))
    in_specs.append(pl.BlockSpec(memory_space=pl.ANY))
    args.append(y)
    return pl.pallas_call(
        functools.partial(_moe_combine_kernel, final=final),
        grid=(T // tm,),
        in_specs=in_specs,
        out_specs=pl.BlockSpec((tm, D), row),
        out_shape=jax.ShapeDtypeStruct((T, D), F32),
        scratch_shapes=[pltpu.VMEM((tm, D), F32), pltpu.VMEM((tm, D), F32), pltpu.SemaphoreType.DMA(())],
        compiler_params=_params("arbitrary"),
        name="moe_combine",
    )(*args)


def _moe_plan(info, te, n_tiles):
    e1 = info[:, 0].astype(jnp.int32)
    e2 = info[:, 1].astype(jnp.int32)
    ids = jnp.arange(N_EXPERTS, dtype=jnp.int32)[None, :]
    oh1 = (e1[:, None] == ids).astype(jnp.int32)
    oh2 = (e2[:, None] == ids).astype(jnp.int32)
    both = oh1 + oh2
    before = jnp.cumsum(both, axis=0) - both
    counts = jnp.sum(both, axis=0)
    padded = (counts + te - 1) // te * te
    ends = jnp.cumsum(padded)
    base = (ends - padded)[None, :] + before
    slot1 = jnp.sum(oh1 * base, axis=1)
    slot2 = jnp.sum(oh2 * base, axis=1)
    n_active = ends[-1] // te
    tile = jnp.arange(n_tiles, dtype=jnp.int32)
    tile_active = (tile < n_active).astype(jnp.int32)
    tile_src = jnp.minimum(tile, n_active - 1)
    tile_expert = jnp.sum((tile_src[:, None] * te >= ends[None, :]).astype(jnp.int32), axis=1)
    tile_expert = jnp.minimum(tile_expert, N_EXPERTS - 1)
    return slot1, slot2, tile_src, tile_expert, tile_active


def moe(x, g, w_router, wg, wu, wd, tm, te, tf, g_final=None):
    T, D = x.shape
    n_tiles = 2 * T // te + N_EXPERTS
    wr = jnp.pad(w_router, ((0, 0), (0, LANE - N_EXPERTS)))
    info = moe_route(x, g, wr, tm)
    slot1, slot2, tile_src, tile_expert, tile_active = _moe_plan(info, te, n_tiles)
    xs = moe_dispatch(x, g, slot1, slot2, n_tiles * te, tm)
    y = moe_experts(xs, wg, wu, wd, tile_src, tile_expert, tile_active, te, tf)
    return moe_combine(x, info, slot1, slot2, y, tm, g_final)


def kernel(x, positions, mla_w_in, mla_g_q, mla_g_kv, mla_w_uq, mla_w_ukv, mla_w_o, kv_g, kv_w, kv_b, swa_w_q, swa_b_q, swa_sinks, swa_w_o, swa_b_o, rel_bias, g_attn, g_ffn, ffn_w_gate, ffn_w_up, ffn_w_down, moe_w_router, moe_w_gate, moe_w_up, moe_w_down, g_final):
    B, S, D = x.shape
    T = B * S
    depth = g_attn.shape[0]
    n_a = mla_w_in.shape[0]
    tm_proj = min(512, T)
    tm_ffn = min(1024, T)
    tq = min(256, S)

    h = x.reshape(T, D)
    pos_col = positions.reshape(T, 1)
    pos_row = positions.reshape(T // BLOCK, 1, BLOCK)
    cos, sin = rope_tables(pos_col, min(1024, T))
    perm = _swa_col_perm()
    zeros_d = jnp.zeros((D,), F32)
    kvw = SWA_KV_HEADS * SWA_HEAD_DIM
    kv = None

    for l in range(depth):
        if l < n_a:
            win, wq1, wq2, wkv = _mla_weights(mla_w_in[l], mla_w_uq[l], mla_w_ukv[l])
            q, k, v = mla_in(h, cos, sin, g_attn[l], win, mla_g_q[l], mla_g_kv[l], wq1, wq2, wkv, tm_proj)
            o = mla_attn(q, k, v, B, S, tq)
            h = proj_res(o, mla_w_o[l].astype(BF16), zeros_d, h, tm_ffn)
        else:
            b = l - n_a
            qproj = (g_attn[l], swa_w_q[b][:, perm].astype(BF16), swa_b_q[b][perm], SWA_HEAD_DIM ** -0.5 * LOG2E)
            if b == 0:
                q, kv = norm_proj(h, [qproj, (kv_g, kv_w.astype(BF16), kv_b, 1.0)], tm_proj)
            else:
                (q,) = norm_proj(h, [qproj], tm_proj)
            o = swa_attn(q, kv, pos_col, pos_row, rel_bias, swa_sinks[b], B, S)
            h = proj_res(o, swa_w_o[b][perm, :].astype(BF16), swa_b_o[b], h, tm_ffn)
        j = l // 2
        gf = g_final if l == depth - 1 else None
        if l % 2 == 0:
            h = ffn(h, g_ffn[l], ffn_w_gate[j].astype(BF16), ffn_w_up[j].astype(BF16),
                    ffn_w_down[j].astype(BF16), tm_ffn, 256, g_final=gf)
        else:
            h = moe(h, g_ffn[l], moe_w_router[j], moe_w_gate[j].astype(BF16), moe_w_up[j].astype(BF16),
                    moe_w_down[j].astype(BF16), tm_proj, tm_ffn, 512, g_final=gf)
    return h.reshape(B, S, D)
```

```python
import functools
import math

import numpy as np
import jax
import jax.numpy as jnp
from jax import lax
from jax.experimental import pallas as pl
from jax.experimental.pallas import tpu as pltpu

MLA_HEADS = 8
QK_NOPE = 128
QK_ROPE = 64
V_HEAD = 128
Q_LORA = 384
KV_LORA = 256
ROPE_THETA = 10000.0
SWA_HEADS = 16
SWA_KV_HEADS = 2
SWA_HEAD_DIM = 64
SWA_GROUPS = SWA_HEADS // SWA_KV_HEADS
WINDOW = 128
BLOCK = 128
REL_BUCKETS = 32
REL_MAX_DIST = 128
N_EXPERTS = 8
EPS = 1e-6
NEG_INF = -1e30
LOG2E = math.log2(math.e)

LANE = 128
V7X_VMEM_BYTES = 64 * 1024 * 1024
VMEM_LIMIT = V7X_VMEM_BYTES - 8 * 1024 * 1024

F32 = jnp.float32
BF16 = jnp.bfloat16


def _params(*sem):
    return pltpu.CompilerParams(dimension_semantics=sem, vmem_limit_bytes=VMEM_LIMIT)


def _rms(x, g):
    return x * lax.rsqrt(jnp.mean(x * x, axis=-1, keepdims=True) + EPS) * g


def _dot(a, b):
    return jnp.dot(a, b, preferred_element_type=F32)


def _dot_nt(a, b):
    return lax.dot_general(a, b, (((1,), (1,)), ((), ())), preferred_element_type=F32)


def _rope_table_kernel(pos_ref, inv_ref, cos_ref, sin_ref):
    ang = pos_ref[...].astype(F32) * inv_ref[...]
    cos_ref[...] = jnp.cos(ang)
    sin_ref[...] = jnp.sin(ang)


def rope_tables(pos_col, tm):
    T = pos_col.shape[0]
    half = QK_ROPE // 2
    inv = np.float32(ROPE_THETA) ** (-np.arange(half, dtype=np.float32) / np.float32(half))
    inv128 = jnp.asarray(np.tile(inv.astype(np.float32), LANE // half)[None, :])
    return pl.pallas_call(
        _rope_table_kernel,
        grid=(T // tm,),
        in_specs=[pl.BlockSpec((tm, 1), lambda i: (i, 0)),
                  pl.BlockSpec((1, LANE), lambda i: (0, 0))],
        out_specs=[pl.BlockSpec((tm, LANE), lambda i: (i, 0)),
                   pl.BlockSpec((tm, LANE), lambda i: (i, 0))],
        out_shape=[jax.ShapeDtypeStruct((T, LANE), F32)] * 2,
        compiler_params=_params("arbitrary"),
        name="rope_tables",
    )(pos_col, inv128)


def _mla_in_kernel(x_ref, cos_ref, sin_ref, ga_ref, win_ref, gq_ref, gkv_ref,
                   wq1_ref, wq2_ref, wkv_ref, q_ref, k_ref, v_ref):
    H = MLA_HEADS
    scale = (QK_NOPE + QK_ROPE) ** -0.5 * LOG2E
    xn = _rms(x_ref[...], ga_ref[...]).astype(BF16)
    lat = _dot(xn, win_ref[...])
    cq = _rms(lat[:, :Q_LORA], gq_ref[...]).astype(BF16)
    ckv = _rms(lat[:, Q_LORA:Q_LORA + KV_LORA], gkv_ref[...]).astype(BF16)
    cos = cos_ref[...]
    sin = sin_ref[...]
    o = Q_LORA + KV_LORA
    kr = (lat[:, o:o + LANE] * cos + lat[:, o + LANE:o + 2 * LANE] * sin).astype(BF16)
    q1 = _dot(cq, wq1_ref[...])
    q2 = _dot(cq, wq2_ref[...])
    kv = _dot(ckv, wkv_ref[...])
    for h in range(H):
        q_ref[:, 2 * LANE * h:2 * LANE * h + LANE] = (q1[:, 2 * LANE * h:2 * LANE * h + LANE] * scale).astype(BF16)
        qr = q1[:, 2 * LANE * h + LANE:2 * LANE * (h + 1)] * cos + q2[:, LANE * h:LANE * (h + 1)] * sin
        q_ref[:, 2 * LANE * h + LANE:2 * LANE * (h + 1)] = (qr * scale).astype(BF16)
        k_ref[:, 2 * LANE * h:2 * LANE * h + LANE] = kv[:, LANE * h:LANE * (h + 1)].astype(BF16)
        k_ref[:, 2 * LANE * h + LANE:2 * LANE * (h + 1)] = kr
    v_ref[...] = kv[:, H * QK_NOPE:].astype(BF16)


def _mla_weights(w_in, w_uq, w_ukv):
    H, half = MLA_HEADS, QK_ROPE // 2
    D = w_in.shape[0]
    o = Q_LORA + KV_LORA
    wr = w_in[:, o:]
    z = jnp.zeros((D, LANE - QK_ROPE), w_in.dtype)
    wr_sw = jnp.concatenate([-wr[:, half:], wr[:, :half]], axis=1)
    win = jnp.concatenate([w_in[:, :o], wr, z, wr_sw, z], axis=1).astype(BF16)
    wq = w_uq.reshape(Q_LORA, H, QK_NOPE + QK_ROPE)
    qn, qr = wq[:, :, :QK_NOPE], wq[:, :, QK_NOPE:]
    zq = jnp.zeros((Q_LORA, H, LANE - QK_ROPE), w_uq.dtype)
    wq1 = jnp.concatenate([qn, qr, zq], axis=2).reshape(Q_LORA, H * 2 * LANE).astype(BF16)
    qr_sw = jnp.concatenate([-qr[:, :, half:], qr[:, :, :half]], axis=2)
    wq2 = jnp.concatenate([qr_sw, zq], axis=2).reshape(Q_LORA, H * LANE).astype(BF16)
    wkv = w_ukv.reshape(KV_LORA, H, QK_NOPE + V_HEAD)
    wkv = jnp.concatenate([wkv[:, :, :QK_NOPE].reshape(KV_LORA, H * QK_NOPE),
                           wkv[:, :, QK_NOPE:].reshape(KV_LORA, H * V_HEAD)], axis=1).astype(BF16)
    return win, wq1, wq2, wkv


def mla_in(x, cos, sin, g_attn, win, g_q, g_kv, wq1, wq2, wkv, tm):
    T, D = x.shape
    H = MLA_HEADS
    row = lambda i: (i, 0)
    full = lambda i: (0, 0)
    return pl.pallas_call(
        _mla_in_kernel,
        grid=(T // tm,),
        in_specs=[pl.BlockSpec((tm, D), row), pl.BlockSpec((tm, LANE), row), pl.BlockSpec((tm, LANE), row),
                  pl.BlockSpec((1, D), full), pl.BlockSpec(win.shape, full),
                  pl.BlockSpec((1, Q_LORA), full), pl.BlockSpec((1, KV_LORA), full),
                  pl.BlockSpec(wq1.shape, full), pl.BlockSpec(wq2.shape, full), pl.BlockSpec(wkv.shape, full)],
        out_specs=[pl.BlockSpec((tm, H * 2 * LANE), row), pl.BlockSpec((tm, H * 2 * LANE), row),
                   pl.BlockSpec((tm, H * V_HEAD), row)],
        out_shape=[jax.ShapeDtypeStruct((T, H * 2 * LANE), BF16), jax.ShapeDtypeStruct((T, H * 2 * LANE), BF16),
                   jax.ShapeDtypeStruct((T, H * V_HEAD), BF16)],
        compiler_params=_params("arbitrary"),
        name="mla_in",
    )(x, cos, sin, g_attn.reshape(1, D), win, g_q.reshape(1, -1), g_kv.reshape(1, -1), wq1, wq2, wkv)


MLA_HEADS_PER_STEP = 2


def _mla_attn_kernel(q_ref, k_ref, v_ref, o_ref, *, tq, nq):
    i = pl.program_id(2)
    r = lax.broadcasted_iota(jnp.int32, (tq, tq), 0)
    c = lax.broadcasted_iota(jnp.int32, (tq, tq), 1)
    QW = 2 * LANE

    def prefix(n_off):
        lo = n_off * tq
        for hh in range(MLA_HEADS_PER_STEP):
            qs, vs = slice(QW * hh, QW * (hh + 1)), slice(V_HEAD * hh, V_HEAD * (hh + 1))
            q = q_ref[:, qs]
            s_d = jnp.where(c <= r, _dot_nt(q, k_ref[lo:lo + tq, qs]), NEG_INF)
            m = jnp.max(s_d, axis=-1, keepdims=True)
            if n_off:
                s_o = _dot_nt(q, k_ref[:lo, qs])
                m = jnp.maximum(m, jnp.max(s_o, axis=-1, keepdims=True))
                p_o = jnp.exp2(s_o - m)
            p_d = jnp.exp2(s_d - m)
            l = jnp.sum(p_d, axis=-1, keepdims=True)
            acc = _dot(p_d.astype(BF16), v_ref[lo:lo + tq, vs])
            if n_off:
                l = l + jnp.sum(p_o, axis=-1, keepdims=True)
                acc = acc + _dot(p_o.astype(BF16), v_ref[:lo, vs])
            o_ref[:, vs] = (acc * (1.0 / l)).astype(o_ref.dtype)

    for n_off in range(nq):
        pl.when(i == n_off)(functools.partial(prefix, n_off))


def mla_attn(q, k, v, B, S, tq):
    G = MLA_HEADS_PER_STEP
    nq = S // tq
    return pl.pallas_call(
        functools.partial(_mla_attn_kernel, tq=tq, nq=nq),
        grid=(B, MLA_HEADS // G, nq),
        in_specs=[pl.BlockSpec((tq, G * 2 * LANE), lambda b, h, i: (b * nq + i, h)),
                  pl.BlockSpec((S, G * 2 * LANE), lambda b, h, i: (b, h)),
                  pl.BlockSpec((S, G * V_HEAD), lambda b, h, i: (b, h))],
        out_specs=pl.BlockSpec((tq, G * V_HEAD), lambda b, h, i: (b * nq + i, h)),
        out_shape=jax.ShapeDtypeStruct((B * S, MLA_HEADS * V_HEAD), BF16),
        compiler_params=_params("arbitrary", "arbitrary", "arbitrary"),
        name="mla_attn",
    )(q, k, v)


def _proj_res_kernel(a_ref, w_ref, b_ref, r_ref, o_ref):
    o_ref[...] = r_ref[...] + _dot(a_ref[...], w_ref[...]) + b_ref[...]


def proj_res(a, w, b, res, tm):
    T, K = a.shape
    N = w.shape[1]
    row = lambda i: (i, 0)
    full = lambda i: (0, 0)
    return pl.pallas_call(
        _proj_res_kernel,
        grid=(T // tm,),
        in_specs=[pl.BlockSpec((tm, K), row), pl.BlockSpec((K, N), full), pl.BlockSpec((1, N), full),
                  pl.BlockSpec((tm, N), row)],
        out_specs=pl.BlockSpec((tm, N), row),
        out_shape=jax.ShapeDtypeStruct((T, N), F32),
        compiler_params=_params("arbitrary"),
        name="proj_res",
    )(a, w, b.reshape(1, N), res)


def _norm_proj_kernel(*refs, n, scales):
    x = refs[0][...]
    outs = refs[1 + 3 * n:]
    for p in range(n):
        g_ref, w_ref, b_ref = refs[1 + 3 * p:4 + 3 * p]
        y = _dot(_rms(x, g_ref[...]).astype(BF16), w_ref[...]) + b_ref[...]
        if scales[p] != 1.0:
            y = y * scales[p]
        outs[p][...] = y.astype(outs[p].dtype)


def norm_proj(x, projs, tm):
    T, D = x.shape
    row = lambda i: (i, 0)
    full = lambda i: (0, 0)
    in_specs = [pl.BlockSpec((tm, D), row)]
    args = [x]
    out_specs, out_shape = [], []
    for g, w, b, _ in projs:
        N = w.shape[1]
        in_specs += [pl.BlockSpec((1, D), full), pl.BlockSpec((D, N), full), pl.BlockSpec((1, N), full)]
        args += [g.reshape(1, D), w, b.reshape(1, N)]
        out_specs.append(pl.BlockSpec((tm, N), row))
        out_shape.append(jax.ShapeDtypeStruct((T, N), BF16))
    return pl.pallas_call(
        functools.partial(_norm_proj_kernel, n=len(projs), scales=tuple(float(p[3]) for p in projs)),
        grid=(T // tm,),
        in_specs=in_specs, out_specs=out_specs, out_shape=out_shape,
        compiler_params=_params("arbitrary"),
        name="norm_proj",
    )(*args)


def _t5_bucket(dist):
    n = jnp.maximum(dist, 0)
    max_exact = REL_BUCKETS // 2
    nf = jnp.maximum(n, 1).astype(F32)
    large = max_exact + (jnp.log(nf / max_exact) / math.log(REL_MAX_DIST / max_exact)
                         * (REL_BUCKETS - max_exact)).astype(jnp.int32)
    large = jnp.minimum(large, REL_BUCKETS - 1)
    return jnp.where(n < max_exact, n, large)


def _fill_swa_table(tab_ref, slot, bucket, valid, col0, bias_ref, sink_ref):
    for kvh in range(SWA_KV_HEADS):
        def body(g, carry, kvh=kvh):
            head = kvh * SWA_GROUPS + g
            t = jnp.zeros(bucket.shape, F32)
            for b in range(REL_BUCKETS):
                t = jnp.where(bucket == b, bias_ref[b, head], t)
            t = jnp.where(valid, t * LOG2E, NEG_INF)
            t = jnp.where(col0, sink_ref[0, head] * LOG2E, t)
            tab_ref[slot, kvh, pl.ds(pl.multiple_of(g * BLOCK, BLOCK), BLOCK), :] = t
            return carry
        lax.fori_loop(0, SWA_GROUPS, body, 0)


def _swa_attn_kernel(bias_ref, sink_ref, q_ref, kvp_ref, kvc_ref, qpos_ref, kpp_ref, kpc_ref,
                     o_ref, tab_ref):
    b_id, i = pl.program_id(0), pl.program_id(1)
    r = lax.broadcasted_iota(jnp.int32, (BLOCK, 2 * BLOCK), 0)
    c = lax.broadcasted_iota(jnp.int32, (BLOCK, 2 * BLOCK), 1)
    rel_idx = r - (c - BLOCK)
    in_window = (rel_idx >= 0) & (rel_idx < WINDOW)
    in_seq = in_window & (c >= BLOCK)
    valid = in_window & ((i * BLOCK + c - BLOCK) >= 0)
    col0 = c == 0

    @pl.when((b_id == 0) & (i == 0))
    def _():
        bucket = _t5_bucket(rel_idx)
        _fill_swa_table(tab_ref, 0, bucket, in_seq, col0, bias_ref, sink_ref)
        _fill_swa_table(tab_ref, 1, bucket, in_window, col0, bias_ref, sink_ref)

    kpos = jnp.concatenate([kpp_ref[0], kpc_ref[0]], axis=1)
    rel_act = qpos_ref[...] - kpos
    mismatch = jnp.where(valid, jnp.where(rel_act != rel_idx, 1, 0), 0)
    consecutive = jnp.max(mismatch) == 0

    @pl.when(jnp.logical_not(consecutive))
    def _():
        _fill_swa_table(tab_ref, 2, _t5_bucket(rel_act), valid, col0, bias_ref, sink_ref)

    slot = jnp.where(consecutive, jnp.where(i == 0, 0, 1), 2)

    kv = jnp.concatenate([kvp_ref[...], kvc_ref[...]], axis=0)
    lane = lax.broadcasted_iota(jnp.int32, (2 * BLOCK, LANE), 1)
    key = lax.broadcasted_iota(jnp.int32, (2 * BLOCK, LANE), 0)
    k_all, v_all = kv[:, :LANE], kv[:, LANE:]
    zero = jnp.zeros_like(k_all)
    q_stack = jnp.concatenate([q_ref[:, LANE * g:LANE * (g + 1)] for g in range(SWA_GROUPS)], axis=0)
    out = jnp.zeros((SWA_GROUPS * BLOCK, LANE), F32)
    for kvh in range(SWA_KV_HEADS):
        mine = (lane < SWA_HEAD_DIM) if kvh == 0 else (lane >= SWA_HEAD_DIM)
        k_h = jnp.where(mine, k_all, zero)
        v_h = jnp.where(mine & (key > 0), v_all, zero)
        s = _dot_nt(q_stack, k_h) + tab_ref[slot, kvh]
        e = jnp.exp2(s - jnp.max(s, axis=-1, keepdims=True))
        denom = jnp.sum(e, axis=-1, keepdims=True)
        out = out + _dot(e.astype(BF16), v_h) * (1.0 / denom)
    for g in range(SWA_GROUPS):
        o_ref[:, LANE * g:LANE * (g + 1)] = out[g * BLOCK:(g + 1) * BLOCK, :].astype(o_ref.dtype)


def swa_attn(q, kv, pos_col, pos_row, rel_bias, sinks, B, S):
    nb = S // BLOCK
    T = B * S
    cur = lambda b, i: (b * nb + i, 0)
    prev = lambda b, i: (b * nb + jnp.maximum(i - 1, 0), 0)
    cur3 = lambda b, i: (b * nb + i, 0, 0)
    prev3 = lambda b, i: (b * nb + jnp.maximum(i - 1, 0), 0, 0)
    smem = pl.BlockSpec(memory_space=pltpu.SMEM)
    return pl.pallas_call(
        _swa_attn_kernel,
        grid=(B, nb),
        in_specs=[smem, smem,
                  pl.BlockSpec((BLOCK, SWA_HEADS * SWA_HEAD_DIM), cur),
                  pl.BlockSpec((BLOCK, 2 * LANE), prev), pl.BlockSpec((BLOCK, 2 * LANE), cur),
                  pl.BlockSpec((BLOCK, 1), cur),
                  pl.BlockSpec((1, 1, BLOCK), prev3), pl.BlockSpec((1, 1, BLOCK), cur3)],
        out_specs=pl.BlockSpec((BLOCK, SWA_HEADS * SWA_HEAD_DIM), cur),
        out_shape=jax.ShapeDtypeStruct((T, SWA_HEADS * SWA_HEAD_DIM), BF16),
        scratch_shapes=[pltpu.VMEM((3, SWA_KV_HEADS, SWA_GROUPS * BLOCK, 2 * BLOCK), F32)],
        compiler_params=_params("arbitrary", "arbitrary"),
        name="swa_attn",
    )(rel_bias, sinks.reshape(1, -1), q, kv, kv, pos_col, pos_row, pos_row)


def _swa_col_perm():
    idx = []
    for p in range(SWA_GROUPS):
        for kvh in range(SWA_KV_HEADS):
            h = kvh * SWA_GROUPS + p
            idx.extend(range(h * SWA_HEAD_DIM, (h + 1) * SWA_HEAD_DIM))
    return np.asarray(idx, dtype=np.int32)


def _swiglu(xb, wg, wu, wd):
    hg = _dot(xb, wg)
    hu = _dot(xb, wu)
    return _dot((hg * jax.nn.sigmoid(hg) * hu).astype(BF16), wd)


def _ffn_kernel(*refs, final):
    it = iter(refs)
    x_ref, g_ref, wg_ref, wu_ref, wd_ref = next(it), next(it), next(it), next(it), next(it)
    gf_ref = next(it) if final else None
    o_ref = next(it)
    x = x_ref[...]
    y = x + _swiglu(_rms(x, g_ref[...]).astype(BF16), wg_ref[...], wu_ref[...], wd_ref[...])
    if final:
        y = _rms(y, gf_ref[...])
    o_ref[...] = y


def _resident(shape, index_map):
    return pl.BlockSpec(shape, index_map, pipeline_mode=pl.Buffered(1))


def ffn(x, g, wg, wu, wd, tm, g_final=None):
    T, D = x.shape
    final = g_final is not None
    row = lambda i: (i, 0)
    full = lambda i: (0, 0)
    in_specs = [pl.BlockSpec((tm, D), row), pl.BlockSpec((1, D), full),
                _resident(wg.shape, full), _resident(wu.shape, full), _resident(wd.shape, full)]
    args = [x, g.reshape(1, D), wg, wu, wd]
    if final:
        in_specs.append(pl.BlockSpec((1, D), full))
        args.append(g_final.reshape(1, D))
    return pl.pallas_call(
        functools.partial(_ffn_kernel, final=final),
        grid=(T // tm,),
        in_specs=in_specs,
        out_specs=pl.BlockSpec((tm, D), row),
        out_shape=jax.ShapeDtypeStruct((T, D), F32),
        compiler_params=_params("arbitrary"),
        name="dense_ffn",
    )(*args)


def _moe_route_kernel(x_ref, g_ref, wr_ref, info_ref):
    xn = _rms(x_ref[...], g_ref[...])
    logits = jnp.dot(xn, wr_ref[...], preferred_element_type=F32, precision=lax.Precision.HIGHEST)
    lane = lax.broadcasted_iota(jnp.int32, logits.shape, 1)
    lg = jnp.where(lane < N_EXPERTS, logits, -jnp.inf)
    m1 = jnp.max(lg, axis=-1, keepdims=True)
    i1 = jnp.min(jnp.where(lg == m1, lane, LANE), axis=-1, keepdims=True)
    lg2 = jnp.where(lane == i1, -jnp.inf, lg)
    m2 = jnp.max(lg2, axis=-1, keepdims=True)
    i2 = jnp.min(jnp.where(lg2 == m2, lane, LANE), axis=-1, keepdims=True)
    e2 = jnp.exp(m2 - m1)
    w1 = 1.0 / (1.0 + e2)
    info = jnp.where(lane == 0, i1.astype(F32), 0.0)
    info = jnp.where(lane == 1, i2.astype(F32), info)
    info = jnp.where(lane == 2, w1, info)
    info_ref[...] = jnp.where(lane == 3, e2 * w1, info)


def moe_route(x, g, wr, tm):
    T, D = x.shape
    row = lambda i: (i, 0)
    full = lambda i: (0, 0)
    return pl.pallas_call(
        _moe_route_kernel,
        grid=(T // tm,),
        in_specs=[pl.BlockSpec((tm, D), row), pl.BlockSpec((1, D), full), pl.BlockSpec((D, LANE), full)],
        out_specs=pl.BlockSpec((tm, LANE), row),
        out_shape=jax.ShapeDtypeStruct((T, LANE), F32),
        compiler_params=_params("arbitrary"),
        name="moe_route",
    )(x, g.reshape(1, D), wr)


ROW_COPY_UNROLL = 8


def _row_copies(src_ref, dst_ref, sem, n, src_row, dst_row):
    def copy(t):
        return pltpu.make_async_copy(src_ref.at[pl.ds(src_row(t), 1)], dst_ref.at[pl.ds(dst_row(t), 1)], sem)

    def start(c, carry):
        for u in range(ROW_COPY_UNROLL):
            copy(c * ROW_COPY_UNROLL + u).start(priority=u % 2)
        return carry

    def wait(c, carry):
        for u in range(ROW_COPY_UNROLL):
            copy(c * ROW_COPY_UNROLL + u).wait()
        return carry

    lax.fori_loop(0, n // ROW_COPY_UNROLL, start, 0)
    lax.fori_loop(0, n // ROW_COPY_UNROLL, wait, 0)


def _moe_dispatch_kernel(pad_ref, s1_ref, s2_ref, x_ref, g_ref, xs_ref, xn_ref, sem):
    tm = xn_ref.shape[0]

    @pl.when(pl.program_id(0) == 0)
    def _():
        xn_ref[...] = jnp.zeros(xn_ref.shape, F32)

        def zero_rows(first, n, rows):
            def copy(r):
                at = first + r * rows
                return pltpu.make_async_copy(xn_ref.at[pl.ds(0, rows)], xs_ref.at[pl.ds(at, rows)], sem)
            lax.fori_loop(0, n, lambda r, c: (copy(r).start(), c)[1], 0)
            lax.fori_loop(0, n, lambda r, c: (copy(r).wait(), c)[1], 0)

        for e in range(N_EXPERTS):
            zero_rows(pad_ref[2 * e], pad_ref[2 * e + 1], 1)
        zero_rows(pl.multiple_of(pad_ref[2 * N_EXPERTS], tm), pad_ref[2 * N_EXPERTS + 1], tm)

    xn_ref[...] = _rms(x_ref[...], g_ref[...])
    for s_ref in (s1_ref, s2_ref):
        _row_copies(xn_ref, xs_ref, sem, tm, lambda t: t, lambda t, s_ref=s_ref: s_ref[0, 0, t])


def moe_dispatch(x, g, slot1, slot2, pad, n_slots, tm):
    T, D = x.shape
    row = lambda i: (i, 0)
    full = lambda i: (0, 0)
    slots = pl.BlockSpec((1, 1, tm), lambda i: (i, 0, 0), memory_space=pltpu.SMEM)
    return pl.pallas_call(
        _moe_dispatch_kernel,
        grid=(T // tm,),
        in_specs=[pl.BlockSpec(memory_space=pltpu.SMEM), slots, slots,
                  pl.BlockSpec((tm, D), row), pl.BlockSpec((1, D), full)],
        out_specs=pl.BlockSpec(memory_space=pl.ANY),
        out_shape=jax.ShapeDtypeStruct((n_slots, D), F32),
        scratch_shapes=[pltpu.VMEM((tm, D), F32), pltpu.SemaphoreType.DMA(())],
        compiler_params=_params("arbitrary"),
        name="moe_dispatch",
    )(pad, slot1.reshape(T // tm, 1, tm), slot2.reshape(T // tm, 1, tm), x, g.reshape(1, D))


def _moe_expert_kernel(src_ref, exp_ref, act_ref, x_ref, wg_ref, wu_ref, wd_ref, y_ref, *, tf):
    del src_ref, exp_ref
    j = pl.program_id(0)

    @pl.when(act_ref[j] == 1)
    def _():
        xb = x_ref[...].astype(BF16)
        y = None
        for lo in range(0, wg_ref.shape[2], tf):
            part = _swiglu(xb, wg_ref[0, :, lo:lo + tf], wu_ref[0, :, lo:lo + tf], wd_ref[0, lo:lo + tf, :])
            y = part if y is None else y + part
        y_ref[...] = y

    @pl.when(act_ref[j] == 0)
    def _():
        y_ref[...] = jnp.zeros(y_ref.shape, F32)


def moe_experts(xs, wg, wu, wd, tile_src, tile_expert, tile_active, te, tf):
    n_slots, D = xs.shape
    Fd = wg.shape[2]
    n_tiles = n_slots // te
    grid_spec = pltpu.PrefetchScalarGridSpec(
        num_scalar_prefetch=3,
        grid=(n_tiles,),
        in_specs=[pl.BlockSpec((te, D), lambda j, src, exp, act: (src[j], 0)),
                  _resident((1, D, Fd), lambda j, src, exp, act: (exp[j], 0, 0)),
                  _resident((1, D, Fd), lambda j, src, exp, act: (exp[j], 0, 0)),
                  _resident((1, Fd, D), lambda j, src, exp, act: (exp[j], 0, 0))],
        out_specs=pl.BlockSpec((te, D), lambda j, src, exp, act: (j, 0)),
    )
    return pl.pallas_call(
        functools.partial(_moe_expert_kernel, tf=tf),
        grid_spec=grid_spec,
        out_shape=jax.ShapeDtypeStruct((n_slots, D), F32),
        compiler_params=_params("arbitrary"),
        name="moe_experts",
    )(tile_src, tile_expert, tile_active, xs, wg, wu, wd)


def _moe_combine_kernel(*refs, final):
    it = iter(refs)
    s1_ref, s2_ref, x_ref, info_ref = next(it), next(it), next(it), next(it)
    gf_ref = next(it) if final else None
    y_ref, o_ref, b1_ref, b2_ref, sem = next(it), next(it), next(it), next(it), next(it)
    tm = b1_ref.shape[0]
    for s_ref, b_ref in ((s1_ref, b1_ref), (s2_ref, b2_ref)):
        _row_copies(y_ref, b_ref, sem, tm, lambda t, s_ref=s_ref: s_ref[0, 0, t], lambda t: t)
    info = info_ref[...]
    y = x_ref[...] + info[:, 2:3] * b1_ref[...] + info[:, 3:4] * b2_ref[...]
    if final:
        y = _rms(y, gf_ref[...])
    o_ref[...] = y


def moe_combine(x, info, slot1, slot2, y, tm, g_final=None):
    T, D = x.shape
    final = g_final is not None
    row = lambda i: (i, 0)
    full = lambda i: (0, 0)
    slots = pl.BlockSpec((1, 1, tm), lambda i: (i, 0, 0), memory_space=pltpu.SMEM)
    in_specs = [slots, slots, pl.BlockSpec((tm, D), row), pl.BlockSpec((tm, LANE), row)]
    args = [slot1.reshape(T // tm, 1, tm), slot2.reshape(T // tm, 1, tm), x, info]
    if final:
        in_specs.append(pl.BlockSpec((1, D), full))
        args.append(g_final.reshape(1, D))
    in_specs.append(pl.BlockSpec(memory_space=pl.ANY))
    args.append(y)
    return pl.pallas_call(
        functools.partial(_moe_combine_kernel, final=final),
        grid=(T // tm,),
        in_specs=in_specs,
        out_specs=pl.BlockSpec((tm, D), row),
        out_shape=jax.ShapeDtypeStruct((T, D), F32),
        scratch_shapes=[pltpu.VMEM((tm, D), F32), pltpu.VMEM((tm, D), F32), pltpu.SemaphoreType.DMA(())],
        compiler_params=_params("arbitrary"),
        name="moe_combine",
    )(*args)


def _moe_plan(info, te, tm, n_tiles):
    e1 = info[:, 0].astype(jnp.int32)
    e2 = info[:, 1].astype(jnp.int32)
    ids = jnp.arange(N_EXPERTS, dtype=jnp.int32)[None, :]
    oh1 = (e1[:, None] == ids).astype(jnp.int32)
    oh2 = (e2[:, None] == ids).astype(jnp.int32)
    both = oh1 + oh2
    before = jnp.cumsum(both, axis=0) - both
    counts = jnp.sum(both, axis=0)
    padded = (counts + te - 1) // te * te
    ends = jnp.cumsum(padded)
    base = (ends - padded)[None, :] + before
    slot1 = jnp.sum(oh1 * base, axis=1)
    slot2 = jnp.sum(oh2 * base, axis=1)
    n_active = ends[-1] // te
    tile = jnp.arange(n_tiles, dtype=jnp.int32)
    tile_active = (tile < n_active).astype(jnp.int32)
    tile_src = jnp.minimum(tile, n_active - 1)
    tile_expert = jnp.sum((tile_src[:, None] * te >= ends[None, :]).astype(jnp.int32), axis=1)
    tile_expert = jnp.minimum(tile_expert, N_EXPERTS - 1)
    pad = jnp.stack([ends - padded + counts, padded - counts], axis=1).reshape(-1)
    idle = jnp.stack([ends[-1], (n_tiles * te - ends[-1]) // tm])
    pad = jnp.concatenate([pad, idle]).astype(jnp.int32)
    return slot1, slot2, tile_src, tile_expert, tile_active, pad


def moe(x, g, w_router, wg, wu, wd, tm, te, tf, g_final=None):
    T, D = x.shape
    assert te % tm == 0
    n_tiles = 2 * T // te + N_EXPERTS
    wr = jnp.pad(w_router, ((0, 0), (0, LANE - N_EXPERTS)))
    info = moe_route(x, g, wr, tm)
    slot1, slot2, tile_src, tile_expert, tile_active, pad = _moe_plan(info, te, tm, n_tiles)
    xs = moe_dispatch(x, g, slot1, slot2, pad, n_tiles * te, tm)
    y = moe_experts(xs, wg, wu, wd, tile_src, tile_expert, tile_active, te, tf)
    return moe_combine(x, info, slot1, slot2, y, tm, g_final)


def kernel(x, positions, mla_w_in, mla_g_q, mla_g_kv, mla_w_uq, mla_w_ukv, mla_w_o, kv_g, kv_w, kv_b, swa_w_q, swa_b_q, swa_sinks, swa_w_o, swa_b_o, rel_bias, g_attn, g_ffn, ffn_w_gate, ffn_w_up, ffn_w_down, moe_w_router, moe_w_gate, moe_w_up, moe_w_down, g_final):
    B, S, D = x.shape
    T = B * S
    depth = g_attn.shape[0]
    n_a = mla_w_in.shape[0]
    tm_proj = min(512, T)
    tm_ffn = min(1024, T)
    tq = min(256, S)

    h = x.reshape(T, D)
    pos_col = positions.reshape(T, 1)
    pos_row = positions.reshape(T // BLOCK, 1, BLOCK)
    cos, sin = rope_tables(pos_col, min(1024, T))
    perm = _swa_col_perm()
    zeros_d = jnp.zeros((D,), F32)
    kvw = SWA_KV_HEADS * SWA_HEAD_DIM
    kv = None

    for l in range(depth):
        if l < n_a:
            win, wq1, wq2, wkv = _mla_weights(mla_w_in[l], mla_w_uq[l], mla_w_ukv[l])
            q, k, v = mla_in(h, cos, sin, g_attn[l], win, mla_g_q[l], mla_g_kv[l], wq1, wq2, wkv, tm_proj)
            o = mla_attn(q, k, v, B, S, tq)
            h = proj_res(o, mla_w_o[l].astype(BF16), zeros_d, h, tm_ffn)
        else:
            b = l - n_a
            qproj = (g_attn[l], swa_w_q[b][:, perm].astype(BF16), swa_b_q[b][perm], SWA_HEAD_DIM ** -0.5 * LOG2E)
            if b == 0:
                q, kv = norm_proj(h, [qproj, (kv_g, kv_w.astype(BF16), kv_b, 1.0)], tm_proj)
            else:
                (q,) = norm_proj(h, [qproj], tm_proj)
            o = swa_attn(q, kv, pos_col, pos_row, rel_bias, swa_sinks[b], B, S)
            h = proj_res(o, swa_w_o[b][perm, :].astype(BF16), swa_b_o[b], h, tm_ffn)
        j = l // 2
        gf = g_final if l == depth - 1 else None
        if l % 2 == 0:
            h = ffn(h, g_ffn[l], ffn_w_gate[j].astype(BF16), ffn_w_up[j].astype(BF16),
                    ffn_w_down[j].astype(BF16), tm_proj, g_final=gf)
        else:
            h = moe(h, g_ffn[l], moe_w_router[j], moe_w_gate[j].astype(BF16), moe_w_up[j].astype(BF16),
                    moe_w_down[j].astype(BF16), tm_proj, tm_ffn, 512, g_final=gf)
    return h.reshape(B, S, D)
```

```python
import functools
import math

import numpy as np
import jax
import jax.numpy as jnp
from jax import lax
from jax.experimental import pallas as pl
from jax.experimental.pallas import tpu as pltpu

MLA_HEADS = 8
QK_NOPE = 128
QK_ROPE = 64
V_HEAD = 128
Q_LORA = 384
KV_LORA = 256
ROPE_THETA = 10000.0
SWA_HEADS = 16
SWA_KV_HEADS = 2
SWA_HEAD_DIM = 64
SWA_GROUPS = SWA_HEADS // SWA_KV_HEADS
WINDOW = 128
BLOCK = 128
REL_BUCKETS = 32
REL_MAX_DIST = 128
N_EXPERTS = 8
EPS = 1e-6
NEG_INF = -1e30
LOG2E = math.log2(math.e)

LANE = 128
V7X_VMEM_BYTES = 64 * 1024 * 1024
VMEM_LIMIT = V7X_VMEM_BYTES - 8 * 1024 * 1024

F32 = jnp.float32
BF16 = jnp.bfloat16


def _params(*sem):
    return pltpu.CompilerParams(dimension_semantics=sem, vmem_limit_bytes=VMEM_LIMIT)


def _rms(x, g):
    return x * lax.rsqrt(jnp.mean(x * x, axis=-1, keepdims=True) + EPS) * g


def _dot(a, b):
    return jnp.dot(a, b, preferred_element_type=F32)


def _dot_nt(a, b):
    return lax.dot_general(a, b, (((1,), (1,)), ((), ())), preferred_element_type=F32)


def _rope_table_kernel(pos_ref, inv_ref, cos_ref, sin_ref):
    ang = pos_ref[...].astype(F32) * inv_ref[...]
    cos_ref[...] = jnp.cos(ang)
    sin_ref[...] = jnp.sin(ang)


def rope_tables(pos_col, tm):
    T = pos_col.shape[0]
    half = QK_ROPE // 2
    inv = np.float32(ROPE_THETA) ** (-np.arange(half, dtype=np.float32) / np.float32(half))
    inv128 = jnp.asarray(np.tile(inv.astype(np.float32), LANE // half)[None, :])
    return pl.pallas_call(
        _rope_table_kernel,
        grid=(T // tm,),
        in_specs=[pl.BlockSpec((tm, 1), lambda i: (i, 0)),
                  pl.BlockSpec((1, LANE), lambda i: (0, 0))],
        out_specs=[pl.BlockSpec((tm, LANE), lambda i: (i, 0)),
                   pl.BlockSpec((tm, LANE), lambda i: (i, 0))],
        out_shape=[jax.ShapeDtypeStruct((T, LANE), F32)] * 2,
        compiler_params=_params("arbitrary"),
        name="rope_tables",
    )(pos_col, inv128)


def _mla_in_kernel(x_ref, cos_ref, sin_ref, ga_ref, win_ref, gq_ref, gkv_ref,
                   wq1_ref, wq2_ref, wkv_ref, q_ref, k_ref, v_ref):
    H = MLA_HEADS
    scale = (QK_NOPE + QK_ROPE) ** -0.5 * LOG2E
    xn = _rms(x_ref[...], ga_ref[...]).astype(BF16)
    lat = _dot(xn, win_ref[...])
    cq = _rms(lat[:, :Q_LORA], gq_ref[...]).astype(BF16)
    ckv = _rms(lat[:, Q_LORA:Q_LORA + KV_LORA], gkv_ref[...]).astype(BF16)
    cos = cos_ref[...]
    sin = sin_ref[...]
    o = Q_LORA + KV_LORA
    kr = (lat[:, o:o + LANE] * cos + lat[:, o + LANE:o + 2 * LANE] * sin).astype(BF16)
    q1 = _dot(cq, wq1_ref[...])
    q2 = _dot(cq, wq2_ref[...])
    kv = _dot(ckv, wkv_ref[...])
    for h in range(H):
        q_ref[:, 2 * LANE * h:2 * LANE * h + LANE] = (q1[:, 2 * LANE * h:2 * LANE * h + LANE] * scale).astype(BF16)
        qr = q1[:, 2 * LANE * h + LANE:2 * LANE * (h + 1)] * cos + q2[:, LANE * h:LANE * (h + 1)] * sin
        q_ref[:, 2 * LANE * h + LANE:2 * LANE * (h + 1)] = (qr * scale).astype(BF16)
        k_ref[:, 2 * LANE * h:2 * LANE * h + LANE] = kv[:, LANE * h:LANE * (h + 1)].astype(BF16)
        k_ref[:, 2 * LANE * h + LANE:2 * LANE * (h + 1)] = kr
    v_ref[...] = kv[:, H * QK_NOPE:].astype(BF16)


def _mla_weights(w_in, w_uq, w_ukv):
    H, half = MLA_HEADS, QK_ROPE // 2
    D = w_in.shape[0]
    o = Q_LORA + KV_LORA
    wr = w_in[:, o:]
    z = jnp.zeros((D, LANE - QK_ROPE), w_in.dtype)
    wr_sw = jnp.concatenate([-wr[:, half:], wr[:, :half]], axis=1)
    win = jnp.concatenate([w_in[:, :o], wr, z, wr_sw, z], axis=1).astype(BF16)
    wq = w_uq.reshape(Q_LORA, H, QK_NOPE + QK_ROPE)
    qn, qr = wq[:, :, :QK_NOPE], wq[:, :, QK_NOPE:]
    zq = jnp.zeros((Q_LORA, H, LANE - QK_ROPE), w_uq.dtype)
    wq1 = jnp.concatenate([qn, qr, zq], axis=2).reshape(Q_LORA, H * 2 * LANE).astype(BF16)
    qr_sw = jnp.concatenate([-qr[:, :, half:], qr[:, :, :half]], axis=2)
    wq2 = jnp.concatenate([qr_sw, zq], axis=2).reshape(Q_LORA, H * LANE).astype(BF16)
    wkv = w_ukv.reshape(KV_LORA, H, QK_NOPE + V_HEAD)
    wkv = jnp.concatenate([wkv[:, :, :QK_NOPE].reshape(KV_LORA, H * QK_NOPE),
                           wkv[:, :, QK_NOPE:].reshape(KV_LORA, H * V_HEAD)], axis=1).astype(BF16)
    return win, wq1, wq2, wkv


def mla_in(x, cos, sin, g_attn, win, g_q, g_kv, wq1, wq2, wkv, tm):
    T, D = x.shape
    H = MLA_HEADS
    row = lambda i: (i, 0)
    full = lambda i: (0, 0)
    return pl.pallas_call(
        _mla_in_kernel,
        grid=(T // tm,),
        in_specs=[pl.BlockSpec((tm, D), row), pl.BlockSpec((tm, LANE), row), pl.BlockSpec((tm, LANE), row),
                  pl.BlockSpec((1, D), full), pl.BlockSpec(win.shape, full),
                  pl.BlockSpec((1, Q_LORA), full), pl.BlockSpec((1, KV_LORA), full),
                  pl.BlockSpec(wq1.shape, full), pl.BlockSpec(wq2.shape, full), pl.BlockSpec(wkv.shape, full)],
        out_specs=[pl.BlockSpec((tm, H * 2 * LANE), row), pl.BlockSpec((tm, H * 2 * LANE), row),
                   pl.BlockSpec((tm, H * V_HEAD), row)],
        out_shape=[jax.ShapeDtypeStruct((T, H * 2 * LANE), BF16), jax.ShapeDtypeStruct((T, H * 2 * LANE), BF16),
                   jax.ShapeDtypeStruct((T, H * V_HEAD), BF16)],
        compiler_params=_params("arbitrary"),
        name="mla_in",
    )(x, cos, sin, g_attn.reshape(1, D), win, g_q.reshape(1, -1), g_kv.reshape(1, -1), wq1, wq2, wkv)


MLA_HEADS_PER_STEP = 4


def _mla_attn_kernel(q_ref, k_ref, v_ref, o_ref, *, tq, nq):
    i = pl.program_id(2)
    r = lax.broadcasted_iota(jnp.int32, (tq, tq), 0)
    c = lax.broadcasted_iota(jnp.int32, (tq, tq), 1)
    QW = 2 * LANE

    def prefix(n_off):
        lo = n_off * tq
        for hh in range(MLA_HEADS_PER_STEP):
            qs, vs = slice(QW * hh, QW * (hh + 1)), slice(V_HEAD * hh, V_HEAD * (hh + 1))
            q = q_ref[:, qs]
            s_d = jnp.where(c <= r, _dot_nt(q, k_ref[lo:lo + tq, qs]), NEG_INF)
            m = jnp.max(s_d, axis=-1, keepdims=True)
            if n_off:
                s_o = _dot_nt(q, k_ref[:lo, qs])
                m = jnp.maximum(m, jnp.max(s_o, axis=-1, keepdims=True))
                p_o = jnp.exp2(s_o - m)
            p_d = jnp.exp2(s_d - m)
            l = jnp.sum(p_d, axis=-1, keepdims=True)
            acc = _dot(p_d.astype(BF16), v_ref[lo:lo + tq, vs])
            if n_off:
                l = l + jnp.sum(p_o, axis=-1, keepdims=True)
                acc = acc + _dot(p_o.astype(BF16), v_ref[:lo, vs])
            o_ref[:, vs] = (acc * (1.0 / l)).astype(o_ref.dtype)

    for n_off in range(nq):
        pl.when(i == n_off)(functools.partial(prefix, n_off))


def mla_attn(q, k, v, B, S, tq):
    G = MLA_HEADS_PER_STEP
    nq = S // tq
    return pl.pallas_call(
        functools.partial(_mla_attn_kernel, tq=tq, nq=nq),
        grid=(B, MLA_HEADS // G, nq),
        in_specs=[pl.BlockSpec((tq, G * 2 * LANE), lambda b, h, i: (b * nq + i, h)),
                  pl.BlockSpec((S, G * 2 * LANE), lambda b, h, i: (b, h)),
                  pl.BlockSpec((S, G * V_HEAD), lambda b, h, i: (b, h))],
        out_specs=pl.BlockSpec((tq, G * V_HEAD), lambda b, h, i: (b * nq + i, h)),
        out_shape=jax.ShapeDtypeStruct((B * S, MLA_HEADS * V_HEAD), BF16),
        compiler_params=_params("arbitrary", "arbitrary", "arbitrary"),
        name="mla_attn",
    )(q, k, v)


def _proj_res_kernel(a_ref, w_ref, b_ref, r_ref, o_ref):
    o_ref[...] = r_ref[...] + _dot(a_ref[...], w_ref[...]) + b_ref[...]


def proj_res(a, w, b, res, tm):
    T, K = a.shape
    N = w.shape[1]
    row = lambda i: (i, 0)
    full = lambda i: (0, 0)
    return pl.pallas_call(
        _proj_res_kernel,
        grid=(T // tm,),
        in_specs=[pl.BlockSpec((tm, K), row), pl.BlockSpec((K, N), full), pl.BlockSpec((1, N), full),
                  pl.BlockSpec((tm, N), row)],
        out_specs=pl.BlockSpec((tm, N), row),
        out_shape=jax.ShapeDtypeStruct((T, N), F32),
        compiler_params=_params("arbitrary"),
        name="proj_res",
    )(a, w, b.reshape(1, N), res)


def _norm_proj_kernel(*refs, n, scales):
    x = refs[0][...]
    outs = refs[1 + 3 * n:]
    for p in range(n):
        g_ref, w_ref, b_ref = refs[1 + 3 * p:4 + 3 * p]
        y = _dot(_rms(x, g_ref[...]).astype(BF16), w_ref[...]) + b_ref[...]
        if scales[p] != 1.0:
            y = y * scales[p]
        outs[p][...] = y.astype(outs[p].dtype)


def norm_proj(x, projs, tm):
    T, D = x.shape
    row = lambda i: (i, 0)
    full = lambda i: (0, 0)
    in_specs = [pl.BlockSpec((tm, D), row)]
    args = [x]
    out_specs, out_shape = [], []
    for g, w, b, _ in projs:
        N = w.shape[1]
        in_specs += [pl.BlockSpec((1, D), full), pl.BlockSpec((D, N), full), pl.BlockSpec((1, N), full)]
        args += [g.reshape(1, D), w, b.reshape(1, N)]
        out_specs.append(pl.BlockSpec((tm, N), row))
        out_shape.append(jax.ShapeDtypeStruct((T, N), BF16))
    return pl.pallas_call(
        functools.partial(_norm_proj_kernel, n=len(projs), scales=tuple(float(p[3]) for p in projs)),
        grid=(T // tm,),
        in_specs=in_specs, out_specs=out_specs, out_shape=out_shape,
        compiler_params=_params("arbitrary"),
        name="norm_proj",
    )(*args)


def _t5_bucket(dist):
    n = jnp.maximum(dist, 0)
    max_exact = REL_BUCKETS // 2
    nf = jnp.maximum(n, 1).astype(F32)
    large = max_exact + (jnp.log(nf / max_exact) / math.log(REL_MAX_DIST / max_exact)
                         * (REL_BUCKETS - max_exact)).astype(jnp.int32)
    large = jnp.minimum(large, REL_BUCKETS - 1)
    return jnp.where(n < max_exact, n, large)


def _fill_swa_table(tab_ref, slot, bucket, valid, col0, bias_ref, sink_ref):
    for kvh in range(SWA_KV_HEADS):
        def body(g, carry, kvh=kvh):
            head = kvh * SWA_GROUPS + g
            t = jnp.zeros(bucket.shape, F32)
            for b in range(REL_BUCKETS):
                t = jnp.where(bucket == b, bias_ref[b, head], t)
            t = jnp.where(valid, t * LOG2E, NEG_INF)
            t = jnp.where(col0, sink_ref[0, head] * LOG2E, t)
            tab_ref[slot, kvh, pl.ds(pl.multiple_of(g * BLOCK, BLOCK), BLOCK), :] = t
            return carry
        lax.fori_loop(0, SWA_GROUPS, body, 0)


def _swa_attn_kernel(bias_ref, sink_ref, q_ref, kvp_ref, kvc_ref, qpos_ref, kpp_ref, kpc_ref,
                     o_ref, tab_ref):
    b_id, i = pl.program_id(0), pl.program_id(1)
    r = lax.broadcasted_iota(jnp.int32, (BLOCK, 2 * BLOCK), 0)
    c = lax.broadcasted_iota(jnp.int32, (BLOCK, 2 * BLOCK), 1)
    rel_idx = r - (c - BLOCK)
    in_window = (rel_idx >= 0) & (rel_idx < WINDOW)
    in_seq = in_window & (c >= BLOCK)
    valid = in_window & ((i * BLOCK + c - BLOCK) >= 0)
    col0 = c == 0

    @pl.when((b_id == 0) & (i == 0))
    def _():
        bucket = _t5_bucket(rel_idx)
        _fill_swa_table(tab_ref, 0, bucket, in_seq, col0, bias_ref, sink_ref)
        _fill_swa_table(tab_ref, 1, bucket, in_window, col0, bias_ref, sink_ref)

    kpos = jnp.concatenate([kpp_ref[0], kpc_ref[0]], axis=1)
    rel_act = qpos_ref[...] - kpos
    mismatch = jnp.where(valid, jnp.where(rel_act != rel_idx, 1, 0), 0)
    consecutive = jnp.max(mismatch) == 0

    @pl.when(jnp.logical_not(consecutive))
    def _():
        _fill_swa_table(tab_ref, 2, _t5_bucket(rel_act), valid, col0, bias_ref, sink_ref)

    slot = jnp.where(consecutive, jnp.where(i == 0, 0, 1), 2)

    kv = jnp.concatenate([kvp_ref[...], kvc_ref[...]], axis=0)
    lane = lax.broadcasted_iota(jnp.int32, (2 * BLOCK, LANE), 1)
    key = lax.broadcasted_iota(jnp.int32, (2 * BLOCK, LANE), 0)
    k_all, v_all = kv[:, :LANE], kv[:, LANE:]
    zero = jnp.zeros_like(k_all)
    q_stack = jnp.concatenate([q_ref[:, LANE * g:LANE * (g + 1)] for g in range(SWA_GROUPS)], axis=0)
    out = jnp.zeros((SWA_GROUPS * BLOCK, LANE), F32)
    for kvh in range(SWA_KV_HEADS):
        mine = (lane < SWA_HEAD_DIM) if kvh == 0 else (lane >= SWA_HEAD_DIM)
        k_h = jnp.where(mine, k_all, zero)
        v_h = jnp.where(mine & (key > 0), v_all, zero)
        s = _dot_nt(q_stack, k_h) + tab_ref[slot, kvh]
        e = jnp.exp2(s - jnp.max(s, axis=-1, keepdims=True))
        denom = jnp.sum(e, axis=-1, keepdims=True)
        out = out + _dot(e.astype(BF16), v_h) * (1.0 / denom)
    for g in range(SWA_GROUPS):
        o_ref[:, LANE * g:LANE * (g + 1)] = out[g * BLOCK:(g + 1) * BLOCK, :].astype(o_ref.dtype)


def swa_attn(q, kv, pos_col, pos_row, rel_bias, sinks, B, S):
    nb = S // BLOCK
    T = B * S
    cur = lambda b, i: (b * nb + i, 0)
    prev = lambda b, i: (b * nb + jnp.maximum(i - 1, 0), 0)
    cur3 = lambda b, i: (b * nb + i, 0, 0)
    prev3 = lambda b, i: (b * nb + jnp.maximum(i - 1, 0), 0, 0)
    smem = pl.BlockSpec(memory_space=pltpu.SMEM)
    return pl.pallas_call(
        _swa_attn_kernel,
        grid=(B, nb),
        in_specs=[smem, smem,
                  pl.BlockSpec((BLOCK, SWA_HEADS * SWA_HEAD_DIM), cur),
                  pl.BlockSpec((BLOCK, 2 * LANE), prev), pl.BlockSpec((BLOCK, 2 * LANE), cur),
                  pl.BlockSpec((BLOCK, 1), cur),
                  pl.BlockSpec((1, 1, BLOCK), prev3), pl.BlockSpec((1, 1, BLOCK), cur3)],
        out_specs=pl.BlockSpec((BLOCK, SWA_HEADS * SWA_HEAD_DIM), cur),
        out_shape=jax.ShapeDtypeStruct((T, SWA_HEADS * SWA_HEAD_DIM), BF16),
        scratch_shapes=[pltpu.VMEM((3, SWA_KV_HEADS, SWA_GROUPS * BLOCK, 2 * BLOCK), F32)],
        compiler_params=_params("arbitrary", "arbitrary"),
        name="swa_attn",
    )(rel_bias, sinks.reshape(1, -1), q, kv, kv, pos_col, pos_row, pos_row)


def _swa_col_perm():
    idx = []
    for p in range(SWA_GROUPS):
        for kvh in range(SWA_KV_HEADS):
            h = kvh * SWA_GROUPS + p
            idx.extend(range(h * SWA_HEAD_DIM, (h + 1) * SWA_HEAD_DIM))
    return np.asarray(idx, dtype=np.int32)


def _swiglu(xb, wg, wu, wd):
    hg = _dot(xb, wg)
    hu = _dot(xb, wu)
    return _dot((hg * jax.nn.sigmoid(hg) * hu).astype(BF16), wd)


def _ffn_kernel(*refs, final):
    it = iter(refs)
    x_ref, g_ref, wg_ref, wu_ref, wd_ref = next(it), next(it), next(it), next(it), next(it)
    gf_ref = next(it) if final else None
    o_ref = next(it)
    x = x_ref[...]
    y = x + _swiglu(_rms(x, g_ref[...]).astype(BF16), wg_ref[...], wu_ref[...], wd_ref[...])
    if final:
        y = _rms(y, gf_ref[...])
    o_ref[...] = y


def _resident(shape, index_map):
    return pl.BlockSpec(shape, index_map, pipeline_mode=pl.Buffered(1))


def ffn(x, g, wg, wu, wd, tm, g_final=None):
    T, D = x.shape
    final = g_final is not None
    row = lambda i: (i, 0)
    full = lambda i: (0, 0)
    in_specs = [pl.BlockSpec((tm, D), row), pl.BlockSpec((1, D), full),
                _resident(wg.shape, full), _resident(wu.shape, full), _resident(wd.shape, full)]
    args = [x, g.reshape(1, D), wg, wu, wd]
    if final:
        in_specs.append(pl.BlockSpec((1, D), full))
        args.append(g_final.reshape(1, D))
    return pl.pallas_call(
        functools.partial(_ffn_kernel, final=final),
        grid=(T // tm,),
        in_specs=in_specs,
        out_specs=pl.BlockSpec((tm, D), row),
        out_shape=jax.ShapeDtypeStruct((T, D), F32),
        compiler_params=_params("arbitrary"),
        name="dense_ffn",
    )(*args)


def _moe_route_kernel(x_ref, g_ref, wr_ref, info_ref):
    xn = _rms(x_ref[...], g_ref[...])
    logits = jnp.dot(xn, wr_ref[...], preferred_element_type=F32, precision=lax.Precision.HIGHEST)
    lane = lax.broadcasted_iota(jnp.int32, logits.shape, 1)
    lg = jnp.where(lane < N_EXPERTS, logits, -jnp.inf)
    m1 = jnp.max(lg, axis=-1, keepdims=True)
    i1 = jnp.min(jnp.where(lg == m1, lane, LANE), axis=-1, keepdims=True)
    lg2 = jnp.where(lane == i1, -jnp.inf, lg)
    m2 = jnp.max(lg2, axis=-1, keepdims=True)
    i2 = jnp.min(jnp.where(lg2 == m2, lane, LANE), axis=-1, keepdims=True)
    e2 = jnp.exp(m2 - m1)
    w1 = 1.0 / (1.0 + e2)
    info = jnp.where(lane == 0, i1.astype(F32), 0.0)
    info = jnp.where(lane == 1, i2.astype(F32), info)
    info = jnp.where(lane == 2, w1, info)
    info_ref[...] = jnp.where(lane == 3, e2 * w1, info)


def moe_route(x, g, wr, tm):
    T, D = x.shape
    row = lambda i: (i, 0)
    full = lambda i: (0, 0)
    return pl.pallas_call(
        _moe_route_kernel,
        grid=(T // tm,),
        in_specs=[pl.BlockSpec((tm, D), row), pl.BlockSpec((1, D), full), pl.BlockSpec((D, LANE), full)],
        out_specs=pl.BlockSpec((tm, LANE), row),
        out_shape=jax.ShapeDtypeStruct((T, LANE), F32),
        compiler_params=_params("arbitrary"),
        name="moe_route",
    )(x, g.reshape(1, D), wr)


SUBLANES = 8


def _row_copies(vmem_ref, hbm_ref, sem, slot_ref, to_hbm):
    def copy(c, u):
        v = vmem_ref.at[c, pl.ds(u, 1)]
        h = hbm_ref.at[pl.ds(slot_ref[0, 0, c * SUBLANES + u], 1)]
        return pltpu.make_async_copy(v, h, sem) if to_hbm else pltpu.make_async_copy(h, v, sem)

    def start(c, carry):
        for u in range(SUBLANES):
            copy(c, u).start()
        return carry

    def wait(c, carry):
        for u in range(SUBLANES):
            copy(c, u).wait()
        return carry

    lax.fori_loop(0, vmem_ref.shape[0], start, 0)
    lax.fori_loop(0, vmem_ref.shape[0], wait, 0)


def _moe_dispatch_kernel(pad_ref, s1_ref, s2_ref, x_ref, g_ref, xs_ref, xn_ref, z_ref, sem):
    tm = x_ref.shape[0]

    @pl.when(pl.program_id(0) == 0)
    def _():
        z_ref[...] = jnp.zeros(z_ref.shape, F32)

        def zero_rows(first, n, rows):
            def copy(r):
                at = first + r * rows
                return pltpu.make_async_copy(z_ref.at[pl.ds(0, rows)], xs_ref.at[pl.ds(at, rows)], sem)
            lax.fori_loop(0, n, lambda r, c: (copy(r).start(), c)[1], 0)
            lax.fori_loop(0, n, lambda r, c: (copy(r).wait(), c)[1], 0)

        for e in range(N_EXPERTS):
            zero_rows(pad_ref[2 * e], pad_ref[2 * e + 1], 1)
        zero_rows(pl.multiple_of(pad_ref[2 * N_EXPERTS], tm), pad_ref[2 * N_EXPERTS + 1], tm)

    xn_ref[...] = _rms(x_ref[...], g_ref[...]).reshape(xn_ref.shape)
    for s_ref in (s1_ref, s2_ref):
        _row_copies(xn_ref, xs_ref, sem, s_ref, to_hbm=True)


def moe_dispatch(x, g, slot1, slot2, pad, n_slots, tm):
    T, D = x.shape
    row = lambda i: (i, 0)
    full = lambda i: (0, 0)
    slots = pl.BlockSpec((1, 1, tm), lambda i: (i, 0, 0), memory_space=pltpu.SMEM)
    return pl.pallas_call(
        _moe_dispatch_kernel,
        grid=(T // tm,),
        in_specs=[pl.BlockSpec(memory_space=pltpu.SMEM), slots, slots,
                  pl.BlockSpec((tm, D), row), pl.BlockSpec((1, D), full)],
        out_specs=pl.BlockSpec(memory_space=pl.ANY),
        out_shape=jax.ShapeDtypeStruct((n_slots, D), F32),
        scratch_shapes=[pltpu.VMEM((tm // SUBLANES, SUBLANES, D), F32), pltpu.VMEM((tm, D), F32),
                        pltpu.SemaphoreType.DMA(())],
        compiler_params=_params("arbitrary"),
        name="moe_dispatch",
    )(pad, slot1.reshape(T // tm, 1, tm), slot2.reshape(T // tm, 1, tm), x, g.reshape(1, D))


def _moe_expert_kernel(src_ref, exp_ref, act_ref, x_ref, wg_ref, wu_ref, wd_ref, y_ref, *, tf):
    del src_ref, exp_ref
    j = pl.program_id(0)

    @pl.when(act_ref[j] == 1)
    def _():
        xb = x_ref[...].astype(BF16)
        y = None
        for lo in range(0, wg_ref.shape[2], tf):
            part = _swiglu(xb, wg_ref[0, :, lo:lo + tf], wu_ref[0, :, lo:lo + tf], wd_ref[0, lo:lo + tf, :])
            y = part if y is None else y + part
        y_ref[...] = y

    @pl.when(act_ref[j] == 0)
    def _():
        y_ref[...] = jnp.zeros(y_ref.shape, F32)


def moe_experts(xs, wg, wu, wd, tile_src, tile_expert, tile_active, te, tf):
    n_slots, D = xs.shape
    Fd = wg.shape[2]
    n_tiles = n_slots // te
    grid_spec = pltpu.PrefetchScalarGridSpec(
        num_scalar_prefetch=3,
        grid=(n_tiles,),
        in_specs=[pl.BlockSpec((te, D), lambda j, src, exp, act: (src[j], 0)),
                  _resident((1, D, Fd), lambda j, src, exp, act: (exp[j], 0, 0)),
                  _resident((1, D, Fd), lambda j, src, exp, act: (exp[j], 0, 0)),
                  _resident((1, Fd, D), lambda j, src, exp, act: (exp[j], 0, 0))],
        out_specs=pl.BlockSpec((te, D), lambda j, src, exp, act: (j, 0)),
    )
    return pl.pallas_call(
        functools.partial(_moe_expert_kernel, tf=tf),
        grid_spec=grid_spec,
        out_shape=jax.ShapeDtypeStruct(xs.shape, F32),
        compiler_params=_params("arbitrary"),
        name="moe_experts",
    )(tile_src, tile_expert, tile_active, xs, wg, wu, wd)


def _moe_combine_kernel(*refs, final):
    it = iter(refs)
    s1_ref, s2_ref, x_ref, info_ref = next(it), next(it), next(it), next(it)
    gf_ref = next(it) if final else None
    y_ref, o_ref, b1_ref, b2_ref, sem = next(it), next(it), next(it), next(it), next(it)
    for s_ref, b_ref in ((s1_ref, b1_ref), (s2_ref, b2_ref)):
        _row_copies(b_ref, y_ref, sem, s_ref, to_hbm=False)
    info = info_ref[...]
    x = x_ref[...]
    y = x + info[:, 2:3] * b1_ref[...].reshape(x.shape) + info[:, 3:4] * b2_ref[...].reshape(x.shape)
    if final:
        y = _rms(y, gf_ref[...])
    o_ref[...] = y


def moe_combine(x, info, slot1, slot2, y, tm, g_final=None):
    T, D = x.shape
    final = g_final is not None
    row = lambda i: (i, 0)
    full = lambda i: (0, 0)
    slots = pl.BlockSpec((1, 1, tm), lambda i: (i, 0, 0), memory_space=pltpu.SMEM)
    in_specs = [slots, slots, pl.BlockSpec((tm, D), row), pl.BlockSpec((tm, LANE), row)]
    args = [slot1.reshape(T // tm, 1, tm), slot2.reshape(T // tm, 1, tm), x, info]
    if final:
        in_specs.append(pl.BlockSpec((1, D), full))
        args.append(g_final.reshape(1, D))
    in_specs.append(pl.BlockSpec(memory_space=pl.ANY))
    args.append(y)
    return pl.pallas_call(
        functools.partial(_moe_combine_kernel, final=final),
        grid=(T // tm,),
        in_specs=in_specs,
        out_specs=pl.BlockSpec((tm, D), row),
        out_shape=jax.ShapeDtypeStruct((T, D), F32),
        scratch_shapes=[pltpu.VMEM((tm // SUBLANES, SUBLANES, D), F32)] * 2 + [pltpu.SemaphoreType.DMA(())],
        compiler_params=_params("arbitrary"),
        name="moe_combine",
    )(*args)


def _moe_plan(info, te, tm, n_tiles):
    e1 = info[:, 0].astype(jnp.int32)
    e2 = info[:, 1].astype(jnp.int32)
    ids = jnp.arange(N_EXPERTS, dtype=jnp.int32)[None, :]
    oh1 = (e1[:, None] == ids).astype(jnp.int32)
    oh2 = (e2[:, None] == ids).astype(jnp.int32)
    both = oh1 + oh2
    before = jnp.cumsum(both, axis=0) - both
    counts = jnp.sum(both, axis=0)
    padded = (counts + te - 1) // te * te
    ends = jnp.cumsum(padded)
    base = (ends - padded)[None, :] + before
    slot1 = jnp.sum(oh1 * base, axis=1)
    slot2 = jnp.sum(oh2 * base, axis=1)
    n_active = ends[-1] // te
    tile = jnp.arange(n_tiles, dtype=jnp.int32)
    tile_active = (tile < n_active).astype(jnp.int32)
    tile_src = jnp.minimum(tile, n_active - 1)
    tile_expert = jnp.sum((tile_src[:, None] * te >= ends[None, :]).astype(jnp.int32), axis=1)
    tile_expert = jnp.minimum(tile_expert, N_EXPERTS - 1)
    pad = jnp.stack([ends - padded + counts, padded - counts], axis=1).reshape(-1)
    idle = jnp.stack([ends[-1], (n_tiles * te - ends[-1]) // tm])
    pad = jnp.concatenate([pad, idle]).astype(jnp.int32)
    return slot1, slot2, tile_src, tile_expert, tile_active, pad


def moe(x, g, w_router, wg, wu, wd, tm, te, tf, g_final=None):
    T, D = x.shape
    assert te % tm == 0
    n_tiles = 2 * T // te + N_EXPERTS
    wr = jnp.pad(w_router, ((0, 0), (0, LANE - N_EXPERTS)))
    info = moe_route(x, g, wr, tm)
    slot1, slot2, tile_src, tile_expert, tile_active, pad = _moe_plan(info, te, tm, n_tiles)
    xs = moe_dispatch(x, g, slot1, slot2, pad, n_tiles * te, tm)
    y = moe_experts(xs, wg, wu, wd, tile_src, tile_expert, tile_active, te, tf)
    return moe_combine(x, info, slot1, slot2, y, tm, g_final)


def kernel(x, positions, mla_w_in, mla_g_q, mla_g_kv, mla_w_uq, mla_w_ukv, mla_w_o, kv_g, kv_w, kv_b, swa_w_q, swa_b_q, swa_sinks, swa_w_o, swa_b_o, rel_bias, g_attn, g_ffn, ffn_w_gate, ffn_w_up, ffn_w_down, moe_w_router, moe_w_gate, moe_w_up, moe_w_down, g_final):
    B, S, D = x.shape
    T = B * S
    depth = g_attn.shape[0]
    n_a = mla_w_in.shape[0]
    tm_proj = min(512, T)
    tm_ffn = min(1024, T)
    tq = min(256, S)

    h = x.reshape(T, D)
    pos_col = positions.reshape(T, 1)
    pos_row = positions.reshape(T // BLOCK, 1, BLOCK)
    cos, sin = rope_tables(pos_col, min(1024, T))
    perm = _swa_col_perm()
    zeros_d = jnp.zeros((D,), F32)
    kvw = SWA_KV_HEADS * SWA_HEAD_DIM
    kv = None

    for l in range(depth):
        if l < n_a:
            win, wq1, wq2, wkv = _mla_weights(mla_w_in[l], mla_w_uq[l], mla_w_ukv[l])
            q, k, v = mla_in(h, cos, sin, g_attn[l], win, mla_g_q[l], mla_g_kv[l], wq1, wq2, wkv, tm_proj)
            o = mla_attn(q, k, v, B, S, tq)
            h = proj_res(o, mla_w_o[l].astype(BF16), zeros_d, h, tm_ffn)
        else:
            b = l - n_a
            qproj = (g_attn[l], swa_w_q[b][:, perm].astype(BF16), swa_b_q[b][perm], SWA_HEAD_DIM ** -0.5 * LOG2E)
            if b == 0:
                q, kv = norm_proj(h, [qproj, (kv_g, kv_w.astype(BF16), kv_b, 1.0)], tm_proj)
            else:
                (q,) = norm_proj(h, [qproj], tm_proj)
            o = swa_attn(q, kv, pos_col, pos_row, rel_bias, swa_sinks[b], B, S)
            h = proj_res(o, swa_w_o[b][perm, :].astype(BF16), swa_b_o[b], h, tm_ffn)
        j = l // 2
        gf = g_final if l == depth - 1 else None
        if l % 2 == 0:
            h = ffn(h, g_ffn[l], ffn_w_gate[j].astype(BF16), ffn_w_up[j].astype(BF16),
                    ffn_w_down[j].astype(BF16), tm_proj, g_final=gf)
        else:
            h = moe(h, g_ffn[l], moe_w_router[j], moe_w_gate[j].astype(BF16), moe_w_up[j].astype(BF16),
                    moe_w_down[j].astype(BF16), tm_proj, tm_ffn, 512, g_final=gf)
    return h.reshape(B, S, D)
```

```python
import functools
import math

import numpy as np
import jax
import jax.numpy as jnp
from jax import lax
from jax.experimental import pallas as pl
from jax.experimental.pallas import tpu as pltpu

MLA_HEADS = 8
QK_NOPE = 128
QK_ROPE = 64
V_HEAD = 128
Q_LORA = 384
KV_LORA = 256
ROPE_THETA = 10000.0
SWA_HEADS = 16
SWA_KV_HEADS = 2
SWA_HEAD_DIM = 64
SWA_GROUPS = SWA_HEADS // SWA_KV_HEADS
WINDOW = 128
BLOCK = 128
REL_BUCKETS = 32
REL_MAX_DIST = 128
N_EXPERTS = 8
EPS = 1e-6
NEG_INF = -1e30
LOG2E = math.log2(math.e)

LANE = 128
V7X_VMEM_BYTES = 64 * 1024 * 1024
VMEM_LIMIT = V7X_VMEM_BYTES - 8 * 1024 * 1024

F32 = jnp.float32
BF16 = jnp.bfloat16


def _params(*sem):
    return pltpu.CompilerParams(dimension_semantics=sem, vmem_limit_bytes=VMEM_LIMIT)


def _rms(x, g):
    return x * lax.rsqrt(jnp.mean(x * x, axis=-1, keepdims=True) + EPS) * g


def _dot(a, b):
    return jnp.dot(a, b, preferred_element_type=F32)


def _dot_nt(a, b):
    return lax.dot_general(a, b, (((1,), (1,)), ((), ())), preferred_element_type=F32)


def _rope_table_kernel(pos_ref, inv_ref, cos_ref, sin_ref):
    ang = pos_ref[...].astype(F32) * inv_ref[...]
    cos_ref[...] = jnp.cos(ang)
    sin_ref[...] = jnp.sin(ang)


def rope_tables(pos_col, tm):
    T = pos_col.shape[0]
    half = QK_ROPE // 2
    inv = np.float32(ROPE_THETA) ** (-np.arange(half, dtype=np.float32) / np.float32(half))
    inv128 = jnp.asarray(np.tile(inv.astype(np.float32), LANE // half)[None, :])
    return pl.pallas_call(
        _rope_table_kernel,
        grid=(T // tm,),
        in_specs=[pl.BlockSpec((tm, 1), lambda i: (i, 0)),
                  pl.BlockSpec((1, LANE), lambda i: (0, 0))],
        out_specs=[pl.BlockSpec((tm, LANE), lambda i: (i, 0)),
                   pl.BlockSpec((tm, LANE), lambda i: (i, 0))],
        out_shape=[jax.ShapeDtypeStruct((T, LANE), F32)] * 2,
        compiler_params=_params("arbitrary"),
        name="rope_tables",
    )(pos_col, inv128)


def _mla_in_kernel(x_ref, cos_ref, sin_ref, ga_ref, win_ref, gq_ref, gkv_ref,
                   wq1_ref, wq2_ref, wkv_ref, q_ref, k_ref, v_ref):
    H = MLA_HEADS
    scale = (QK_NOPE + QK_ROPE) ** -0.5 * LOG2E
    xn = _rms(x_ref[...], ga_ref[...]).astype(BF16)
    lat = _dot(xn, win_ref[...])
    cq = _rms(lat[:, :Q_LORA], gq_ref[...]).astype(BF16)
    ckv = _rms(lat[:, Q_LORA:Q_LORA + KV_LORA], gkv_ref[...]).astype(BF16)
    cos = cos_ref[...]
    sin = sin_ref[...]
    o = Q_LORA + KV_LORA
    kr = (lat[:, o:o + LANE] * cos + lat[:, o + LANE:o + 2 * LANE] * sin).astype(BF16)
    q1 = _dot(cq, wq1_ref[...])
    q2 = _dot(cq, wq2_ref[...])
    kv = _dot(ckv, wkv_ref[...])
    for h in range(H):
        q_ref[:, 2 * LANE * h:2 * LANE * h + LANE] = (q1[:, 2 * LANE * h:2 * LANE * h + LANE] * scale).astype(BF16)
        qr = q1[:, 2 * LANE * h + LANE:2 * LANE * (h + 1)] * cos + q2[:, LANE * h:LANE * (h + 1)] * sin
        q_ref[:, 2 * LANE * h + LANE:2 * LANE * (h + 1)] = (qr * scale).astype(BF16)
        k_ref[:, 2 * LANE * h:2 * LANE * h + LANE] = kv[:, LANE * h:LANE * (h + 1)].astype(BF16)
        k_ref[:, 2 * LANE * h + LANE:2 * LANE * (h + 1)] = kr
    v_ref[...] = kv[:, H * QK_NOPE:].astype(BF16)


def _mla_weights(w_in, w_uq, w_ukv):
    H, half = MLA_HEADS, QK_ROPE // 2
    D = w_in.shape[0]
    o = Q_LORA + KV_LORA
    wr = w_in[:, o:]
    z = jnp.zeros((D, LANE - QK_ROPE), w_in.dtype)
    wr_sw = jnp.concatenate([-wr[:, half:], wr[:, :half]], axis=1)
    win = jnp.concatenate([w_in[:, :o], wr, z, wr_sw, z], axis=1).astype(BF16)
    wq = w_uq.reshape(Q_LORA, H, QK_NOPE + QK_ROPE)
    qn, qr = wq[:, :, :QK_NOPE], wq[:, :, QK_NOPE:]
    zq = jnp.zeros((Q_LORA, H, LANE - QK_ROPE), w_uq.dtype)
    wq1 = jnp.concatenate([qn, qr, zq], axis=2).reshape(Q_LORA, H * 2 * LANE).astype(BF16)
    qr_sw = jnp.concatenate([-qr[:, :, half:], qr[:, :, :half]], axis=2)
    wq2 = jnp.concatenate([qr_sw, zq], axis=2).reshape(Q_LORA, H * LANE).astype(BF16)
    wkv = w_ukv.reshape(KV_LORA, H, QK_NOPE + V_HEAD)
    wkv = jnp.concatenate([wkv[:, :, :QK_NOPE].reshape(KV_LORA, H * QK_NOPE),
                           wkv[:, :, QK_NOPE:].reshape(KV_LORA, H * V_HEAD)], axis=1).astype(BF16)
    return win, wq1, wq2, wkv


def mla_in(x, cos, sin, g_attn, win, g_q, g_kv, wq1, wq2, wkv, tm):
    T, D = x.shape
    H = MLA_HEADS
    row = lambda i: (i, 0)
    full = lambda i: (0, 0)
    return pl.pallas_call(
        _mla_in_kernel,
        grid=(T // tm,),
        in_specs=[pl.BlockSpec((tm, D), row), pl.BlockSpec((tm, LANE), row), pl.BlockSpec((tm, LANE), row),
                  pl.BlockSpec((1, D), full), pl.BlockSpec(win.shape, full),
                  pl.BlockSpec((1, Q_LORA), full), pl.BlockSpec((1, KV_LORA), full),
                  pl.BlockSpec(wq1.shape, full), pl.BlockSpec(wq2.shape, full), pl.BlockSpec(wkv.shape, full)],
        out_specs=[pl.BlockSpec((tm, H * 2 * LANE), row), pl.BlockSpec((tm, H * 2 * LANE), row),
                   pl.BlockSpec((tm, H * V_HEAD), row)],
        out_shape=[jax.ShapeDtypeStruct((T, H * 2 * LANE), BF16), jax.ShapeDtypeStruct((T, H * 2 * LANE), BF16),
                   jax.ShapeDtypeStruct((T, H * V_HEAD), BF16)],
        compiler_params=_params("arbitrary"),
        name="mla_in",
    )(x, cos, sin, g_attn.reshape(1, D), win, g_q.reshape(1, -1), g_kv.reshape(1, -1), wq1, wq2, wkv)


MLA_HEADS_PER_STEP = 4


def _mla_attn_kernel(q_ref, k_ref, v_ref, o_ref, *, tq, nq):
    i = pl.program_id(2)
    r = lax.broadcasted_iota(jnp.int32, (tq, tq), 0)
    c = lax.broadcasted_iota(jnp.int32, (tq, tq), 1)
    QW = 2 * LANE

    def prefix(n_off):
        lo = n_off * tq
        for hh in range(MLA_HEADS_PER_STEP):
            qs, vs = slice(QW * hh, QW * (hh + 1)), slice(V_HEAD * hh, V_HEAD * (hh + 1))
            q = q_ref[:, qs]
            s_d = jnp.where(c <= r, _dot_nt(q, k_ref[lo:lo + tq, qs]), NEG_INF)
            m = jnp.max(s_d, axis=-1, keepdims=True)
            if n_off:
                s_o = _dot_nt(q, k_ref[:lo, qs])
                m = jnp.maximum(m, jnp.max(s_o, axis=-1, keepdims=True))
                p_o = jnp.exp2(s_o - m)
            p_d = jnp.exp2(s_d - m)
            l = jnp.sum(p_d, axis=-1, keepdims=True)
            acc = _dot(p_d.astype(BF16), v_ref[lo:lo + tq, vs])
            if n_off:
                l = l + jnp.sum(p_o, axis=-1, keepdims=True)
                acc = acc + _dot(p_o.astype(BF16), v_ref[:lo, vs])
            o_ref[:, vs] = (acc * (1.0 / l)).astype(o_ref.dtype)

    for n_off in range(nq):
        pl.when(i == n_off)(functools.partial(prefix, n_off))


def mla_attn(q, k, v, B, S, tq):
    G = MLA_HEADS_PER_STEP
    nq = S // tq
    return pl.pallas_call(
        functools.partial(_mla_attn_kernel, tq=tq, nq=nq),
        grid=(B, MLA_HEADS // G, nq),
        in_specs=[pl.BlockSpec((tq, G * 2 * LANE), lambda b, h, i: (b * nq + i, h)),
                  pl.BlockSpec((S, G * 2 * LANE), lambda b, h, i: (b, h)),
                  pl.BlockSpec((S, G * V_HEAD), lambda b, h, i: (b, h))],
        out_specs=pl.BlockSpec((tq, G * V_HEAD), lambda b, h, i: (b * nq + i, h)),
        out_shape=jax.ShapeDtypeStruct((B * S, MLA_HEADS * V_HEAD), BF16),
        compiler_params=_params("arbitrary", "arbitrary", "arbitrary"),
        name="mla_attn",
    )(q, k, v)


def _norm_proj_kernel(*refs, n, scales):
    x = refs[0][...]
    outs = refs[1 + 3 * n:]
    for p in range(n):
        g_ref, w_ref, b_ref = refs[1 + 3 * p:4 + 3 * p]
        y = _dot(_rms(x, g_ref[...]).astype(BF16), w_ref[...]) + b_ref[...]
        if scales[p] != 1.0:
            y = y * scales[p]
        outs[p][...] = y.astype(outs[p].dtype)


def norm_proj(x, projs, tm):
    T, D = x.shape
    row = lambda i: (i, 0)
    full = lambda i: (0, 0)
    in_specs = [pl.BlockSpec((tm, D), row)]
    args = [x]
    out_specs, out_shape = [], []
    for g, w, b, _ in projs:
        N = w.shape[1]
        in_specs += [pl.BlockSpec((1, D), full), pl.BlockSpec((D, N), full), pl.BlockSpec((1, N), full)]
        args += [g.reshape(1, D), w, b.reshape(1, N)]
        out_specs.append(pl.BlockSpec((tm, N), row))
        out_shape.append(jax.ShapeDtypeStruct((T, N), BF16))
    return pl.pallas_call(
        functools.partial(_norm_proj_kernel, n=len(projs), scales=tuple(float(p[3]) for p in projs)),
        grid=(T // tm,),
        in_specs=in_specs, out_specs=out_specs, out_shape=out_shape,
        compiler_params=_params("arbitrary"),
        name="norm_proj",
    )(*args)


def _t5_bucket(dist):
    n = jnp.maximum(dist, 0)
    max_exact = REL_BUCKETS // 2
    nf = jnp.maximum(n, 1).astype(F32)
    large = max_exact + (jnp.log(nf / max_exact) / math.log(REL_MAX_DIST / max_exact)
                         * (REL_BUCKETS - max_exact)).astype(jnp.int32)
    large = jnp.minimum(large, REL_BUCKETS - 1)
    return jnp.where(n < max_exact, n, large)


def _fill_swa_table(tab_ref, slot, bucket, valid, col0, bias_ref, sink_ref):
    for kvh in range(SWA_KV_HEADS):
        def body(g, carry, kvh=kvh):
            head = kvh * SWA_GROUPS + g
            t = jnp.zeros(bucket.shape, F32)
            for b in range(REL_BUCKETS):
                t = jnp.where(bucket == b, bias_ref[b, head], t)
            t = jnp.where(valid, t * LOG2E, NEG_INF)
            t = jnp.where(col0, sink_ref[0, head] * LOG2E, t)
            tab_ref[slot, kvh, pl.ds(pl.multiple_of(g * BLOCK, BLOCK), BLOCK), :] = t
            return carry
        lax.fori_loop(0, SWA_GROUPS, body, 0)


def _swa_attn_kernel(bias_ref, sink_ref, q_ref, kvp_ref, kvc_ref, qpos_ref, kpp_ref, kpc_ref,
                     o_ref, tab_ref):
    b_id, i = pl.program_id(0), pl.program_id(1)
    r = lax.broadcasted_iota(jnp.int32, (BLOCK, 2 * BLOCK), 0)
    c = lax.broadcasted_iota(jnp.int32, (BLOCK, 2 * BLOCK), 1)
    rel_idx = r - (c - BLOCK)
    in_window = (rel_idx >= 0) & (rel_idx < WINDOW)
    in_seq = in_window & (c >= BLOCK)
    valid = in_window & ((i * BLOCK + c - BLOCK) >= 0)
    col0 = c == 0

    @pl.when((b_id == 0) & (i == 0))
    def _():
        bucket = _t5_bucket(rel_idx)
        _fill_swa_table(tab_ref, 0, bucket, in_seq, col0, bias_ref, sink_ref)
        _fill_swa_table(tab_ref, 1, bucket, in_window, col0, bias_ref, sink_ref)

    kpos = jnp.concatenate([kpp_ref[0], kpc_ref[0]], axis=1)
    rel_act = qpos_ref[...] - kpos
    mismatch = jnp.where(valid, jnp.where(rel_act != rel_idx, 1, 0), 0)
    consecutive = jnp.max(mismatch) == 0

    @pl.when(jnp.logical_not(consecutive))
    def _():
        _fill_swa_table(tab_ref, 2, _t5_bucket(rel_act), valid, col0, bias_ref, sink_ref)

    slot = jnp.where(consecutive, jnp.where(i == 0, 0, 1), 2)

    kv = jnp.concatenate([kvp_ref[...], kvc_ref[...]], axis=0)
    lane = lax.broadcasted_iota(jnp.int32, (2 * BLOCK, LANE), 1)
    key = lax.broadcasted_iota(jnp.int32, (2 * BLOCK, LANE), 0)
    k_all, v_all = kv[:, :LANE], kv[:, LANE:]
    zero = jnp.zeros_like(k_all)
    q_stack = jnp.concatenate([q_ref[:, LANE * g:LANE * (g + 1)] for g in range(SWA_GROUPS)], axis=0)
    out = jnp.zeros((SWA_GROUPS * BLOCK, LANE), F32)
    for kvh in range(SWA_KV_HEADS):
        mine = (lane < SWA_HEAD_DIM) if kvh == 0 else (lane >= SWA_HEAD_DIM)
        k_h = jnp.where(mine, k_all, zero)
        v_h = jnp.where(mine & (key > 0), v_all, zero)
        s = _dot_nt(q_stack, k_h) + tab_ref[slot, kvh]
        e = jnp.exp2(s - jnp.max(s, axis=-1, keepdims=True))
        denom = jnp.sum(e, axis=-1, keepdims=True)
        out = out + _dot(e.astype(BF16), v_h) * (1.0 / denom)
    for g in range(SWA_GROUPS):
        o_ref[:, LANE * g:LANE * (g + 1)] = out[g * BLOCK:(g + 1) * BLOCK, :].astype(o_ref.dtype)


def swa_attn(q, kv, pos_col, pos_row, rel_bias, sinks, B, S):
    nb = S // BLOCK
    T = B * S
    cur = lambda b, i: (b * nb + i, 0)
    prev = lambda b, i: (b * nb + jnp.maximum(i - 1, 0), 0)
    cur3 = lambda b, i: (b * nb + i, 0, 0)
    prev3 = lambda b, i: (b * nb + jnp.maximum(i - 1, 0), 0, 0)
    smem = pl.BlockSpec(memory_space=pltpu.SMEM)
    return pl.pallas_call(
        _swa_attn_kernel,
        grid=(B, nb),
        in_specs=[smem, smem,
                  pl.BlockSpec((BLOCK, SWA_HEADS * SWA_HEAD_DIM), cur),
                  pl.BlockSpec((BLOCK, 2 * LANE), prev), pl.BlockSpec((BLOCK, 2 * LANE), cur),
                  pl.BlockSpec((BLOCK, 1), cur),
                  pl.BlockSpec((1, 1, BLOCK), prev3), pl.BlockSpec((1, 1, BLOCK), cur3)],
        out_specs=pl.BlockSpec((BLOCK, SWA_HEADS * SWA_HEAD_DIM), cur),
        out_shape=jax.ShapeDtypeStruct((T, SWA_HEADS * SWA_HEAD_DIM), BF16),
        scratch_shapes=[pltpu.VMEM((3, SWA_KV_HEADS, SWA_GROUPS * BLOCK, 2 * BLOCK), F32)],
        compiler_params=_params("arbitrary", "arbitrary"),
        name="swa_attn",
    )(rel_bias, sinks.reshape(1, -1), q, kv, kv, pos_col, pos_row, pos_row)


def _swa_col_perm():
    idx = []
    for p in range(SWA_GROUPS):
        for kvh in range(SWA_KV_HEADS):
            h = kvh * SWA_GROUPS + p
            idx.extend(range(h * SWA_HEAD_DIM, (h + 1) * SWA_HEAD_DIM))
    return np.asarray(idx, dtype=np.int32)


def _swiglu(xb, wg, wu, wd):
    hg = _dot(xb, wg)
    hu = _dot(xb, wu)
    return _dot((hg * jax.nn.sigmoid(hg) * hu).astype(BF16), wd)


def _attn_residual(res_ref, a_ref, wo_ref, bo_ref):
    return res_ref[...] + _dot(a_ref[...], wo_ref[...]) + bo_ref[...]


def _attn_specs(a, wo, bo, tm):
    D = wo.shape[1]
    specs = [pl.BlockSpec((tm, a.shape[1]), lambda i: (i, 0)), _resident(wo.shape, lambda i: (0, 0)),
             pl.BlockSpec((1, D), lambda i: (0, 0))]
    return specs, [a, wo, bo.reshape(1, D)]


def _attn_residual_kernel(res_ref, a_ref, wo_ref, bo_ref, o_ref):
    o_ref[...] = _attn_residual(res_ref, a_ref, wo_ref, bo_ref)


def attn_residual(res, attn, tm):
    T, D = res.shape
    attn_specs, attn_args = _attn_specs(*attn, tm)
    return pl.pallas_call(
        _attn_residual_kernel,
        grid=(T // tm,),
        in_specs=[pl.BlockSpec((tm, D), lambda i: (i, 0))] + attn_specs,
        out_specs=pl.BlockSpec((tm, D), lambda i: (i, 0)),
        out_shape=jax.ShapeDtypeStruct((T, D), F32),
        compiler_params=_params("arbitrary"),
        name="attn_residual",
    )(res, *attn_args)


def _ffn_kernel(*refs, final):
    it = iter(refs)
    res_ref, a_ref, wo_ref, bo_ref = next(it), next(it), next(it), next(it)
    g_ref, wg_ref, wu_ref, wd_ref = next(it), next(it), next(it), next(it)
    gf_ref = next(it) if final else None
    o_ref = next(it)
    x = _attn_residual(res_ref, a_ref, wo_ref, bo_ref)
    y = x + _swiglu(_rms(x, g_ref[...]).astype(BF16), wg_ref[...], wu_ref[...], wd_ref[...])
    if final:
        y = _rms(y, gf_ref[...])
    o_ref[...] = y


def _resident(shape, index_map):
    return pl.BlockSpec(shape, index_map, pipeline_mode=pl.Buffered(1))


def ffn(res, attn, g, wg, wu, wd, tm, g_final=None):
    T, D = res.shape
    final = g_final is not None
    row = lambda i: (i, 0)
    full = lambda i: (0, 0)
    attn_specs, attn_args = _attn_specs(*attn, tm)
    in_specs = [pl.BlockSpec((tm, D), row)] + attn_specs + [
        pl.BlockSpec((1, D), full), _resident(wg.shape, full), _resident(wu.shape, full), _resident(wd.shape, full)]
    args = [res] + attn_args + [g.reshape(1, D), wg, wu, wd]
    if final:
        in_specs.append(pl.BlockSpec((1, D), full))
        args.append(g_final.reshape(1, D))
    return pl.pallas_call(
        functools.partial(_ffn_kernel, final=final),
        grid=(T // tm,),
        in_specs=in_specs,
        out_specs=pl.BlockSpec((tm, D), row),
        out_shape=jax.ShapeDtypeStruct((T, D), F32),
        compiler_params=_params("arbitrary"),
        name="dense_ffn",
    )(*args)


def _moe_route_kernel(x_ref, g_ref, wh_ref, wl_ref, info_ref):
    xn = _rms(x_ref[...], g_ref[...])
    xh = xn.astype(BF16)
    xl = (xn - xh.astype(F32)).astype(BF16)
    logits = _dot(xh, wh_ref[...]) + (_dot(xh, wl_ref[...]) + _dot(xl, wh_ref[...]))
    lane = lax.broadcasted_iota(jnp.int32, logits.shape, 1)
    lg = jnp.where(lane < N_EXPERTS, logits, -jnp.inf)
    m1 = jnp.max(lg, axis=-1, keepdims=True)
    i1 = jnp.min(jnp.where(lg == m1, lane, LANE), axis=-1, keepdims=True)
    lg2 = jnp.where(lane == i1, -jnp.inf, lg)
    m2 = jnp.max(lg2, axis=-1, keepdims=True)
    i2 = jnp.min(jnp.where(lg2 == m2, lane, LANE), axis=-1, keepdims=True)
    e2 = jnp.exp(m2 - m1)
    w1 = 1.0 / (1.0 + e2)
    info = jnp.where(lane == 0, i1.astype(F32), 0.0)
    info = jnp.where(lane == 1, i2.astype(F32), info)
    info = jnp.where(lane == 2, w1, info)
    info_ref[...] = jnp.where(lane == 3, e2 * w1, info)


def moe_route(x, g, wr, tm):
    T, D = x.shape
    row = lambda i: (i, 0)
    full = lambda i: (0, 0)
    wh = wr.astype(BF16)
    wl = (wr - wh.astype(F32)).astype(BF16)
    return pl.pallas_call(
        _moe_route_kernel,
        grid=(T // tm,),
        in_specs=[pl.BlockSpec((tm, D), row), pl.BlockSpec((1, D), full),
                  pl.BlockSpec((D, LANE), full), pl.BlockSpec((D, LANE), full)],
        out_specs=pl.BlockSpec((tm, LANE), row),
        out_shape=jax.ShapeDtypeStruct((T, LANE), F32),
        compiler_params=_params("arbitrary"),
        name="moe_route",
    )(x, g.reshape(1, D), wh, wl)


SUBLANES = 8


def _row_copies(vmem_ref, hbm_ref, sem, slot_ref, to_hbm):
    def copy(c, u):
        v = vmem_ref.at[c, pl.ds(u, 1)]
        h = hbm_ref.at[pl.ds(slot_ref[0, 0, c * SUBLANES + u], 1)]
        return pltpu.make_async_copy(v, h, sem) if to_hbm else pltpu.make_async_copy(h, v, sem)

    def start(c, carry):
        for u in range(SUBLANES):
            copy(c, u).start()
        return carry

    def wait(c, carry):
        for u in range(SUBLANES):
            copy(c, u).wait()
        return carry

    lax.fori_loop(0, vmem_ref.shape[0], start, 0)
    lax.fori_loop(0, vmem_ref.shape[0], wait, 0)


def _moe_dispatch_kernel(pad_ref, s1_ref, s2_ref, x_ref, g_ref, xs_ref, xn_ref, z_ref, sem):
    tm = x_ref.shape[0]

    @pl.when(pl.program_id(0) == 0)
    def _():
        z_ref[...] = jnp.zeros(z_ref.shape, F32)

        def zero_rows(first, n, rows):
            def copy(r):
                at = first + r * rows
                return pltpu.make_async_copy(z_ref.at[pl.ds(0, rows)], xs_ref.at[pl.ds(at, rows)], sem)
            lax.fori_loop(0, n, lambda r, c: (copy(r).start(), c)[1], 0)
            lax.fori_loop(0, n, lambda r, c: (copy(r).wait(), c)[1], 0)

        for e in range(N_EXPERTS):
            zero_rows(pad_ref[2 * e], pad_ref[2 * e + 1], 1)
        zero_rows(pl.multiple_of(pad_ref[2 * N_EXPERTS], tm), pad_ref[2 * N_EXPERTS + 1], tm)

    xn_ref[...] = _rms(x_ref[...], g_ref[...]).reshape(xn_ref.shape)
    for s_ref in (s1_ref, s2_ref):
        _row_copies(xn_ref, xs_ref, sem, s_ref, to_hbm=True)


def moe_dispatch(x, g, slot1, slot2, pad, n_slots, tm):
    T, D = x.shape
    row = lambda i: (i, 0)
    full = lambda i: (0, 0)
    slots = pl.BlockSpec((1, 1, tm), lambda i: (i, 0, 0), memory_space=pltpu.SMEM)
    return pl.pallas_call(
        _moe_dispatch_kernel,
        grid=(T // tm,),
        in_specs=[pl.BlockSpec(memory_space=pltpu.SMEM), slots, slots,
                  pl.BlockSpec((tm, D), row), pl.BlockSpec((1, D), full)],
        out_specs=pl.BlockSpec(memory_space=pl.ANY),
        out_shape=jax.ShapeDtypeStruct((n_slots, D), F32),
        scratch_shapes=[pltpu.VMEM((tm // SUBLANES, SUBLANES, D), F32), pltpu.VMEM((tm, D), F32),
                        pltpu.SemaphoreType.DMA(())],
        compiler_params=_params("arbitrary"),
        name="moe_dispatch",
    )(pad, slot1.reshape(T // tm, 1, tm), slot2.reshape(T // tm, 1, tm), x, g.reshape(1, D))


def _moe_expert_kernel(src_ref, exp_ref, act_ref, x_ref, wg_ref, wu_ref, wd_ref, y_ref, *, tf):
    del src_ref, exp_ref
    j = pl.program_id(0)

    @pl.when(act_ref[j] == 1)
    def _():
        xb = x_ref[...].astype(BF16)
        y = None
        for lo in range(0, wg_ref.shape[2], tf):
            part = _swiglu(xb, wg_ref[0, :, lo:lo + tf], wu_ref[0, :, lo:lo + tf], wd_ref[0, lo:lo + tf, :])
            y = part if y is None else y + part
        y_ref[...] = y

    @pl.when(act_ref[j] == 0)
    def _():
        y_ref[...] = jnp.zeros(y_ref.shape, F32)


def moe_experts(xs, wg, wu, wd, tile_src, tile_expert, tile_active, te, tf):
    n_slots, D = xs.shape
    Fd = wg.shape[2]
    n_tiles = n_slots // te
    grid_spec = pltpu.PrefetchScalarGridSpec(
        num_scalar_prefetch=3,
        grid=(n_tiles,),
        in_specs=[pl.BlockSpec((te, D), lambda j, src, exp, act: (src[j], 0)),
                  _resident((1, D, Fd), lambda j, src, exp, act: (exp[j], 0, 0)),
                  _resident((1, D, Fd), lambda j, src, exp, act: (exp[j], 0, 0)),
                  _resident((1, Fd, D), lambda j, src, exp, act: (exp[j], 0, 0))],
        out_specs=pl.BlockSpec((te, D), lambda j, src, exp, act: (j, 0)),
    )
    return pl.pallas_call(
        functools.partial(_moe_expert_kernel, tf=tf),
        grid_spec=grid_spec,
        out_shape=jax.ShapeDtypeStruct(xs.shape, F32),
        compiler_params=_params("arbitrary"),
        name="moe_experts",
    )(tile_src, tile_expert, tile_active, xs, wg, wu, wd)


def _moe_combine_kernel(*refs, final):
    it = iter(refs)
    s1_ref, s2_ref, x_ref, info_ref = next(it), next(it), next(it), next(it)
    gf_ref = next(it) if final else None
    y_ref, o_ref, b1_ref, b2_ref, sem = next(it), next(it), next(it), next(it), next(it)
    for s_ref, b_ref in ((s1_ref, b1_ref), (s2_ref, b2_ref)):
        _row_copies(b_ref, y_ref, sem, s_ref, to_hbm=False)
    info = info_ref[...]
    x = x_ref[...]
    y = x + info[:, 2:3] * b1_ref[...].reshape(x.shape) + info[:, 3:4] * b2_ref[...].reshape(x.shape)
    if final:
        y = _rms(y, gf_ref[...])
    o_ref[...] = y


def moe_combine(x, info, slot1, slot2, y, tm, g_final=None):
    T, D = x.shape
    final = g_final is not None
    row = lambda i: (i, 0)
    full = lambda i: (0, 0)
    slots = pl.BlockSpec((1, 1, tm), lambda i: (i, 0, 0), memory_space=pltpu.SMEM)
    in_specs = [slots, slots, pl.BlockSpec((tm, D), row), pl.BlockSpec((tm, LANE), row)]
    args = [slot1.reshape(T // tm, 1, tm), slot2.reshape(T // tm, 1, tm), x, info]
    if final:
        in_specs.append(pl.BlockSpec((1, D), full))
        args.append(g_final.reshape(1, D))
    in_specs.append(pl.BlockSpec(memory_space=pl.ANY))
    args.append(y)
    return pl.pallas_call(
        functools.partial(_moe_combine_kernel, final=final),
        grid=(T // tm,),
        in_specs=in_specs,
        out_specs=pl.BlockSpec((tm, D), row),
        out_shape=jax.ShapeDtypeStruct((T, D), F32),
        scratch_shapes=[pltpu.VMEM((tm // SUBLANES, SUBLANES, D), F32)] * 2 + [pltpu.SemaphoreType.DMA(())],
        compiler_params=_params("arbitrary"),
        name="moe_combine",
    )(*args)


def _moe_plan(info, te, tm, n_tiles):
    e1 = info[:, 0].astype(jnp.int32)
    e2 = info[:, 1].astype(jnp.int32)
    ids = jnp.arange(N_EXPERTS, dtype=jnp.int32)[None, :]
    oh1 = (e1[:, None] == ids).astype(jnp.int32)
    oh2 = (e2[:, None] == ids).astype(jnp.int32)
    both = oh1 + oh2
    before = jnp.cumsum(both, axis=0) - both
    counts = jnp.sum(both, axis=0)
    padded = (counts + te - 1) // te * te
    ends = jnp.cumsum(padded)
    base = (ends - padded)[None, :] + before
    slot1 = jnp.sum(oh1 * base, axis=1)
    slot2 = jnp.sum(oh2 * base, axis=1)
    n_active = ends[-1] // te
    tile = jnp.arange(n_tiles, dtype=jnp.int32)
    tile_active = (tile < n_active).astype(jnp.int32)
    tile_src = jnp.minimum(tile, n_active - 1)
    tile_expert = jnp.sum((tile_src[:, None] * te >= ends[None, :]).astype(jnp.int32), axis=1)
    tile_expert = jnp.minimum(tile_expert, N_EXPERTS - 1)
    pad = jnp.stack([ends - padded + counts, padded - counts], axis=1).reshape(-1)
    idle = jnp.stack([ends[-1], (n_tiles * te - ends[-1]) // tm])
    pad = jnp.concatenate([pad, idle]).astype(jnp.int32)
    return slot1, slot2, tile_src, tile_expert, tile_active, pad


def moe(x, g, w_router, wg, wu, wd, tm, te, tf, g_final=None):
    T, D = x.shape
    assert te % tm == 0
    n_tiles = 2 * T // te + N_EXPERTS
    wr = jnp.pad(w_router, ((0, 0), (0, LANE - N_EXPERTS)))
    info = moe_route(x, g, wr, tm)
    slot1, slot2, tile_src, tile_expert, tile_active, pad = _moe_plan(info, te, tm, n_tiles)
    xs = moe_dispatch(x, g, slot1, slot2, pad, n_tiles * te, tm)
    y = moe_experts(xs, wg, wu, wd, tile_src, tile_expert, tile_active, te, tf)
    return moe_combine(x, info, slot1, slot2, y, tm, g_final)


def kernel(x, positions, mla_w_in, mla_g_q, mla_g_kv, mla_w_uq, mla_w_ukv, mla_w_o, kv_g, kv_w, kv_b, swa_w_q, swa_b_q, swa_sinks, swa_w_o, swa_b_o, rel_bias, g_attn, g_ffn, ffn_w_gate, ffn_w_up, ffn_w_down, moe_w_router, moe_w_gate, moe_w_up, moe_w_down, g_final):
    B, S, D = x.shape
    T = B * S
    depth = g_attn.shape[0]
    n_a = mla_w_in.shape[0]
    tm_proj = min(512, T)
    tm_ffn = min(1024, T)
    tq = min(256, S)

    h = x.reshape(T, D)
    pos_col = positions.reshape(T, 1)
    pos_row = positions.reshape(T // BLOCK, 1, BLOCK)
    cos, sin = rope_tables(pos_col, min(1024, T))
    perm = _swa_col_perm()
    zeros_d = jnp.zeros((D,), F32)
    kv = None

    for l in range(depth):
        if l < n_a:
            win, wq1, wq2, wkv = _mla_weights(mla_w_in[l], mla_w_uq[l], mla_w_ukv[l])
            q, k, v = mla_in(h, cos, sin, g_attn[l], win, mla_g_q[l], mla_g_kv[l], wq1, wq2, wkv, tm_proj)
            attn = (mla_attn(q, k, v, B, S, tq), mla_w_o[l].astype(BF16), zeros_d)
        else:
            b = l - n_a
            qproj = (g_attn[l], swa_w_q[b][:, perm].astype(BF16), swa_b_q[b][perm], SWA_HEAD_DIM ** -0.5 * LOG2E)
            if b == 0:
                q, kv = norm_proj(h, [qproj, (kv_g, kv_w.astype(BF16), kv_b, 1.0)], tm_proj)
            else:
                (q,) = norm_proj(h, [qproj], tm_proj)
            o = swa_attn(q, kv, pos_col, pos_row, rel_bias, swa_sinks[b], B, S)
            attn = (o, swa_w_o[b][perm, :].astype(BF16), swa_b_o[b])
        j = l // 2
        gf = g_final if l == depth - 1 else None
        if l % 2 == 0:
            h = ffn(h, attn, g_ffn[l], ffn_w_gate[j].astype(BF16), ffn_w_up[j].astype(BF16),
                    ffn_w_down[j].astype(BF16), tm_proj, g_final=gf)
        else:
            h = attn_residual(h, attn, tm_ffn)
            h = moe(h, g_ffn[l], moe_w_router[j], moe_w_gate[j].astype(BF16), moe_w_up[j].astype(BF16),
                    moe_w_down[j].astype(BF16), tm_proj, tm_ffn, 512, g_final=gf)
    return h.reshape(B, S, D)
```
